```python
import jax, jax.numpy as jnp
from jax import lax
import numpy as np

D_MODEL = 2048
BATCH = 4
SEQ = 4096
DEPTH = 2

GRID_W = 64
CTX_LEN = 256
HEAD_DIM = 128
BLOCK = 128
WINDOW = 128
A_HEADS = 6
A_KV = 2
B_GROUPS = 4
B_CHUNK = 128
C_HEADS = 6
C_KV = 2
A_WIDTH = A_HEADS * HEAD_DIM
B_WIDTH = B_GROUPS * HEAD_DIM
C_WIDTH = C_HEADS * HEAD_DIM
MIX_WIDTH = A_WIDTH + B_WIDTH + C_WIDTH
IN_SPLITS = (A_WIDTH, A_KV * HEAD_DIM, A_KV * HEAD_DIM, B_WIDTH, B_WIDTH, C_WIDTH, C_KV * HEAD_DIM, C_KV * HEAD_DIM)
IN_WIDTH = A_WIDTH + 2 * A_KV * HEAD_DIM + 2 * B_WIDTH + C_WIDTH + 2 * C_KV * HEAD_DIM
OUT_SPLITS = (A_WIDTH, B_WIDTH, C_WIDTH)
N_EXPERTS = 16
EC_CAPACITY = 2
D_EXPERT = 2048
N_MOD = 6
ROPE_THETA = 10000.0
EPS = 1e-6
NEG_INF = -1e30

kernel_name = "hymba_style_diffusion_trunk_ec_moe"


def _split(x, sizes):
    idx = [int(i) for i in np.cumsum(sizes)[:-1]]
    return jnp.split(x, idx, axis=-1)


def rms_norm(x, g):
    xf = x.astype(jnp.float32)
    y = xf * lax.rsqrt(jnp.mean(xf * xf, axis=-1, keepdims=True) + EPS)
    return (y * g.astype(jnp.float32)).astype(x.dtype)


def modulate(h, shift, scale):
    return h * (1 + scale) + shift


def axial_angles(n):
    rows = n // GRID_W
    row = jnp.repeat(jnp.arange(rows, dtype=jnp.float32), GRID_W)
    col = (jnp.arange(n) % GRID_W).astype(jnp.float32)
    n_freq = HEAD_DIM // 4
    inv = ROPE_THETA ** (-jnp.arange(n_freq, dtype=jnp.float32) / n_freq)
    return row[:, None] * inv, col[:, None] * inv


def rope_half(x, ang):
    f = ang.shape[-1]
    cos = jnp.cos(ang)[None, :, None, :].astype(x.dtype)
    sin = jnp.sin(ang)[None, :, None, :].astype(x.dtype)
    x1, x2 = x[..., :f], x[..., f:]
    return jnp.concatenate([x1 * cos - x2 * sin, x2 * cos + x1 * sin], axis=-1)


def rope_2d(x, ang_r, ang_c):
    h = x.shape[-1] // 2
    return jnp.concatenate([rope_half(x[..., :h], ang_r), rope_half(x[..., h:], ang_c)], axis=-1)


def attn_heads(p_q, p_k, p_v, n_q, n_kv, qn, kn, ang):
    b, n = p_q.shape[:2]
    q = rms_norm(p_q.reshape(b, n, n_q, HEAD_DIM), qn)
    k = rms_norm(p_k.reshape(b, n, n_kv, HEAD_DIM), kn)
    if ang is not None:
        q = rope_2d(q, *ang)
        k = rope_2d(k, *ang)
    v = p_v.reshape(b, n, n_kv, HEAD_DIM)
    return q.reshape(b, n, n_kv, n_q // n_kv, HEAD_DIM), k, v


def dense_attn(q, k, v, sink):
    s = jnp.einsum('bqkgd,bskd->bkgqs', q, k).astype(jnp.float32) * (HEAD_DIM ** -0.5)
    if sink is not None:
        sk = jnp.broadcast_to(sink[None, :, :, None, None].astype(jnp.float32), s.shape[:-1] + (1,))
        s = jnp.concatenate([sk, s], axis=-1)
    p = jax.nn.softmax(s, axis=-1)
    if sink is not None:
        p = p[..., 1:]
    return jnp.einsum('bkgqs,bskd->bqkgd', p.astype(v.dtype), v)


def window_attn_latent(q, k, v, k_ctx, v_ctx, sink):
    b, n, kv, g, dh = q.shape
    nb = n // BLOCK
    L = k_ctx.shape[1]
    qb = q.reshape(b, nb, BLOCK, kv, g, dh)
    pad = ((0, 0), (BLOCK, BLOCK), (0, 0), (0, 0))
    kp, vp = jnp.pad(k, pad), jnp.pad(v, pad)
    kb = jnp.concatenate([kp[:, j * BLOCK:j * BLOCK + n].reshape(b, nb, BLOCK, kv, dh) for j in range(3)], axis=2)
    vb = jnp.concatenate([vp[:, j * BLOCK:j * BLOCK + n].reshape(b, nb, BLOCK, kv, dh) for j in range(3)], axis=2)
    scale = HEAD_DIM ** -0.5
    s_band = jnp.einsum('bnqkgd,bnskd->bnkgqs', qb, kb).astype(jnp.float32) * scale
    s_ctx = jnp.einsum('bnqkgd,bskd->bnkgqs', qb, k_ctx).astype(jnp.float32) * scale
    blk = jnp.arange(nb)[:, None, None]
    qpos = blk * BLOCK + jnp.arange(BLOCK)[None, :, None]
    kpos = (blk - 1) * BLOCK + jnp.arange(3 * BLOCK)[None, None, :]
    valid = (jnp.abs(kpos - qpos) <= WINDOW) & (kpos >= 0) & (kpos < n)
    s_band = jnp.where(valid[None, :, None, None], s_band, NEG_INF)
    sk = jnp.broadcast_to(sink[None, None, :, :, None, None].astype(jnp.float32), s_band.shape[:-1] + (1,))
    p = jax.nn.softmax(jnp.concatenate([sk, s_ctx, s_band], axis=-1), axis=-1).astype(v.dtype)
    o = (jnp.einsum('bnkgqs,bskd->bnqkgd', p[..., 1:1 + L], v_ctx)
         + jnp.einsum('bnkgqs,bnskd->bnqkgd', p[..., 1 + L:], vb))
    return o.reshape(b, n, kv * g * dh)


def global_attn_latent(q, k_all, v_all):
    b, n, kv, g, dh = q.shape
    nb = n // BLOCK
    qb = q.reshape(b, nb, BLOCK, kv, g, dh).transpose(1, 0, 2, 3, 4, 5)
    o = lax.map(lambda qi: dense_attn(qi, k_all, v_all, None), qb)
    return o.transpose(1, 0, 2, 3, 4, 5).reshape(b, n, kv * g * dh)


def chunk_mlp(u, v, vn_g, w_s, b_s):
    b, n, _ = u.shape
    nc = n // B_CHUNK
    u = jax.nn.gelu(u).reshape(b, nc, B_CHUNK, B_GROUPS, HEAD_DIM)
    v = rms_norm(jax.nn.gelu(v).reshape(b, n, B_GROUPS, HEAD_DIM), vn_g.reshape(B_GROUPS, HEAD_DIM))
    v = v.reshape(b, nc, B_CHUNK, B_GROUPS, HEAD_DIM)
    mixed = jnp.einsum('gpq,bnqgd->bnpgd', w_s, v) + b_s.T[None, None, :, :, None]
    return (u * mixed).reshape(b, n, B_WIDTH)


def merge_heads(o_a, o_b, o_c, out_g, w_out):
    g_a, g_b, g_c = _split(out_g, OUT_SPLITS)
    y = jnp.concatenate([rms_norm(o_a, g_a), rms_norm(o_b, g_b), rms_norm(o_c, g_c)], axis=-1)
    return y @ w_out


def token_mixers(hx, hc, ang, w_in, qn_a, kn_a, sink_a, vn_b, w_s, b_s, qn_c, kn_c, out_g, w_out, need_ctx):
    qa_x, ka_x, va_x, ub_x, vb_x, qc_x, kc_x, vc_x = _split(hx @ w_in, IN_SPLITS)
    qa_c, ka_c, va_c, ub_c, vb_c, qc_c, kc_c, vc_c = _split(hc @ w_in, IN_SPLITS)
    sink = sink_a.reshape(A_KV, A_HEADS // A_KV)
    q_a, k_a, v_a = attn_heads(qa_x, ka_x, va_x, A_HEADS, A_KV, qn_a, kn_a, ang)
    q_ac, k_ac, v_ac = attn_heads(qa_c, ka_c, va_c, A_HEADS, A_KV, qn_a, kn_a, None)
    o_a = window_attn_latent(q_a, k_a, v_a, k_ac, v_ac, sink)
    o_b = chunk_mlp(ub_x, vb_x, vn_b, w_s, b_s)
    q_c, k_c, v_c = attn_heads(qc_x, kc_x, vc_x, C_HEADS, C_KV, qn_c, kn_c, ang)
    q_cc, k_cc, v_cc = attn_heads(qc_c, kc_c, vc_c, C_HEADS, C_KV, qn_c, kn_c, None)
    k_all = jnp.concatenate([k_cc, k_c], axis=1)
    v_all = jnp.concatenate([v_cc, v_c], axis=1)
    o_c = global_attn_latent(q_c, k_all, v_all)
    y_x = merge_heads(o_a, o_b, o_c, out_g, w_out)
    y_c = None
    if need_ctx:
        bsz, L = hc.shape[:2]
        o_ac = dense_attn(q_ac, k_ac, v_ac, sink).reshape(bsz, L, A_WIDTH)
        o_bc = chunk_mlp(ub_c, vb_c, vn_b, w_s, b_s)
        o_cc = dense_attn(q_cc, k_cc, v_cc, None).reshape(bsz, L, C_WIDTH)
        y_c = merge_heads(o_ac, o_bc, o_cc, out_g, w_out)
    return y_x, y_c


def expert_choice_ffn(h, w_router, w_gate, w_up, w_down):
    n, d = h.shape[1], h.shape[2]
    cap = EC_CAPACITY * n // N_EXPERTS

    def per_sample(hs):
        aff = jax.nn.softmax((hs @ w_router).astype(jnp.float32), axis=-1)
        gate, idx = lax.top_k(aff.T, cap)
        xg = hs[idx]
        a = jnp.einsum('ecd,edf->ecf', xg, w_gate)
        u = jnp.einsum('ecd,edf->ecf', xg, w_up)
        y = jnp.einsum('ecf,efd->ecd', jax.nn.silu(a) * u, w_down)
        y = y * gate[..., None].astype(y.dtype)
        return jnp.zeros_like(hs).at[idx.reshape(-1)].add(y.reshape(-1, d))

    return jax.vmap(per_sample)(h)


def setup_inputs(seed: int = 0) -> dict:
    key = jax.random.key(seed)
    ks = jax.random.split(key, 24)
    f32 = jnp.float32
    nrm = lambda k, shape, s: (jax.random.normal(k, shape, f32) * s)
    gain = lambda k, shape: 1.0 + 0.02 * jax.random.normal(k, shape, f32)
    return {
        "x": nrm(ks[0], (BATCH, SEQ, D_MODEL), 1.0),
        "c": nrm(ks[1], (BATCH, D_MODEL), 1.0),
        "ctx": nrm(ks[2], (BATCH, CTX_LEN, D_MODEL), 1.0),
        "c_ctx": nrm(ks[3], (D_MODEL,), 1.0),
        "w_mod": nrm(ks[4], (DEPTH, D_MODEL, N_MOD * D_MODEL), 0.5 * D_MODEL ** -0.5),
        "b_mod": nrm(ks[5], (DEPTH, N_MOD * D_MODEL), 0.01),
        "norm1_g": gain(ks[6], (DEPTH, D_MODEL)),
        "norm2_g": gain(ks[7], (DEPTH, D_MODEL)),
        "w_in": nrm(ks[8], (DEPTH, D_MODEL, IN_WIDTH), D_MODEL ** -0.5),
        "qn_a": gain(ks[9], (DEPTH, HEAD_DIM)),
        "kn_a": gain(ks[10], (DEPTH, HEAD_DIM)),
        "sink_a": nrm(ks[11], (DEPTH, A_HEADS), 0.5),
        "vn_b": gain(ks[12], (DEPTH, B_WIDTH)),
        "w_s": nrm(ks[13], (DEPTH, B_GROUPS, B_CHUNK, B_CHUNK), 0.5 * B_CHUNK ** -0.5),
        "b_s": 1.0 + nrm(ks[14], (DEPTH, B_GROUPS, B_CHUNK), 0.1),
        "qn_c": gain(ks[15], (DEPTH, HEAD_DIM)),
        "kn_c": gain(ks[16], (DEPTH, HEAD_DIM)),
        "out_g": gain(ks[17], (DEPTH, MIX_WIDTH)),
        "w_out": nrm(ks[18], (DEPTH, MIX_WIDTH, D_MODEL), MIX_WIDTH ** -0.5),
        "w_router": nrm(ks[19], (DEPTH, D_MODEL, N_EXPERTS), D_MODEL ** -0.5),
        "w_gate": nrm(ks[20], (DEPTH, N_EXPERTS, D_MODEL, D_EXPERT), D_MODEL ** -0.5),
        "w_up": nrm(ks[21], (DEPTH, N_EXPERTS, D_MODEL, D_EXPERT), D_MODEL ** -0.5),
        "w_down": nrm(ks[22], (DEPTH, N_EXPERTS, D_EXPERT, D_MODEL), D_EXPERT ** -0.5),
    }


def reference(x, c, ctx, c_ctx, w_mod, b_mod, norm1_g, norm2_g, w_in, qn_a, kn_a, sink_a, vn_b,
              w_s, b_s, qn_c, kn_c, out_g, w_out, w_router, w_gate, w_up, w_down):
    ang = axial_angles(x.shape[1])
    xc = ctx
    for i in range(DEPTH):
        last = i == DEPTH - 1
        mod_x = (jax.nn.silu(c) @ w_mod[i] + b_mod[i])[:, None, :]
        mod_c = (jax.nn.silu(c_ctx) @ w_mod[i] + b_mod[i])[None, None, :]
        sh1, sc1, g1, sh2, sc2, g2 = jnp.split(mod_x, N_MOD, axis=-1)
        sh1c, sc1c, g1c, sh2c, sc2c, g2c = jnp.split(mod_c, N_MOD, axis=-1)
        hx = modulate(rms_norm(x, norm1_g[i]), sh1, sc1)
        hc = modulate(rms_norm(xc, norm1_g[i]), sh1c, sc1c)
        y_x, y_c = token_mixers(hx, hc, ang, w_in[i], qn_a[i], kn_a[i], sink_a[i], vn_b[i], w_s[i], b_s[i],
                                qn_c[i], kn_c[i], out_g[i], w_out[i], not last)
        x = x + g1 * y_x
        hx = modulate(rms_norm(x, norm2_g[i]), sh2, sc2)
        x = x + g2 * expert_choice_ffn(hx, w_router[i], w_gate[i], w_up[i], w_down[i])
        if not last:
            xc = xc + g1c * y_c
            hc = modulate(rms_norm(xc, norm2_g[i]), sh2c, sc2c)
            xc = xc + g2c * expert_choice_ffn(hc, w_router[i], w_gate[i], w_up[i], w_down[i])
    return x
```

```python
import functools

import jax
import jax.numpy as jnp
import numpy as np
from jax import lax
from jax.experimental import pallas as pl
from jax.experimental.pallas import tpu as pltpu

F32 = jnp.float32
BF16 = jnp.bfloat16
I32 = jnp.int32

HEAD_DIM = 128
GRID_W = 64
WINDOW = 128
A_HEADS, A_KV = 6, 2
B_GROUPS, B_CHUNK = 4, 128
C_HEADS, C_KV = 6, 2
A_WIDTH = A_HEADS * HEAD_DIM
B_WIDTH = B_GROUPS * HEAD_DIM
C_WIDTH = C_HEADS * HEAD_DIM
KV_WIDTH = A_KV * HEAD_DIM
OFF_QA = 0
OFF_KA = OFF_QA + A_WIDTH
OFF_VA = OFF_KA + KV_WIDTH
OFF_UB = OFF_VA + KV_WIDTH
OFF_VB = OFF_UB + B_WIDTH
OFF_QC = OFF_VB + B_WIDTH
OFF_KC = OFF_QC + C_WIDTH
OFF_VC = OFF_KC + KV_WIDTH
IN_WIDTH = OFF_VC + KV_WIDTH
EC_CAPACITY = 2
N_MOD = 6
ROPE_THETA = 10000.0
EPS = 1e-6
NEG_INF = -1e30
QK_SCALE = HEAD_DIM ** -0.5
MOD_ROWS = 8

VMEM_LIMIT = 56 * 1024 * 1024


def _params(sem, vmem=VMEM_LIMIT):
    return pltpu.CompilerParams(dimension_semantics=sem, vmem_limit_bytes=vmem)


def _pick_tile(target, *sizes):
    t = target
    while any(s % t for s in sizes):
        t //= 2
    return t


def _rms_scale(x):
    return x * lax.rsqrt(jnp.mean(x * x, axis=-1, keepdims=True) + EPS)


def _dot_nt(a, b):
    return lax.dot_general(a, b, (((1,), (1,)), ((), ())), preferred_element_type=F32)


def _mod_kernel(c_ref, w_ref, b_ref, o_ref):
    c = c_ref[...]
    s = c / (1.0 + jnp.exp(-c))
    o_ref[...] = jnp.dot(s, w_ref[...], precision=lax.Precision.HIGHEST,
                         preferred_element_type=F32) + b_ref[...]


def _modulation(cvec, w_mod, b_mod, layer):
    d = cvec.shape[1]
    n = w_mod.shape[2]
    tn = _pick_tile(1024, n)
    return pl.pallas_call(
        _mod_kernel,
        grid=(n // tn,),
        in_specs=[
            pl.BlockSpec((MOD_ROWS, d), lambda j: (0, 0)),
            pl.BlockSpec((None, d, tn), lambda j: (layer, 0, j)),
            pl.BlockSpec((None, 1, tn), lambda j: (layer, 0, j)),
        ],
        out_specs=pl.BlockSpec((MOD_ROWS, tn), lambda j: (0, j)),
        out_shape=jax.ShapeDtypeStruct((MOD_ROWS, n), F32),
        compiler_params=_params(("arbitrary",)),
        name="modulation",
    )(cvec, w_mod, b_mod.reshape(b_mod.shape[0], 1, n))


def _inproj_kernel(x_ref, sh_ref, sc_ref, g_ref, w_ref, cos_ref, s1_ref, s2_ref,
                   qna_ref, kna_ref, qnc_ref, knc_ref, vnb_ref, ws_ref, bsb_ref,
                   qa_ref, ka_ref, va_ref, qc_ref, kc_ref, vc_ref, ob_ref, *, tm):
    x = x_ref[...]
    h = _rms_scale(x) * g_ref[...]
    h = h * (1.0 + sc_ref[0]) + sh_ref[0]
    hb = h.astype(BF16)
    cos, s1, s2 = cos_ref[...], s1_ref[...], s2_ref[...]

    def proj(off, width):
        return jnp.dot(hb, w_ref[:, off:off + width], preferred_element_type=F32)

    def qk_head(p, gain, scale):
        q = _rms_scale(p) * gain
        q = q * cos + pltpu.roll(q, 32, 1) * s1 + pltpu.roll(q, 96, 1) * s2
        if scale != 1.0:
            q = q * scale
        return q.astype(BF16)

    def heads(off, n_heads, gain_ref, scale, out_ref):
        p = proj(off, n_heads * HEAD_DIM)
        for hd in range(n_heads):
            sl = slice(hd * HEAD_DIM, (hd + 1) * HEAD_DIM)
            out_ref[:, sl] = qk_head(p[:, sl], gain_ref[...], scale)

    heads(OFF_QA, A_HEADS, qna_ref, QK_SCALE, qa_ref)
    heads(OFF_KA, A_KV, kna_ref, 1.0, ka_ref)
    va_ref[...] = proj(OFF_VA, KV_WIDTH).astype(BF16)
    heads(OFF_QC, C_HEADS, qnc_ref, QK_SCALE, qc_ref)
    heads(OFF_KC, C_KV, knc_ref, 1.0, kc_ref)
    vc_ref[...] = proj(OFF_VC, KV_WIDTH).astype(BF16)

    pu = proj(OFF_UB, B_WIDTH)
    pv = proj(OFF_VB, B_WIDTH)
    for g in range(B_GROUPS):
        sl = slice(g * HEAD_DIM, (g + 1) * HEAD_DIM)
        u = jax.nn.gelu(pu[:, sl])
        vn = (_rms_scale(jax.nn.gelu(pv[:, sl])) * vnb_ref[:, sl]).astype(BF16)
        for c in range(tm // B_CHUNK):
            rows = slice(c * B_CHUNK, (c + 1) * B_CHUNK)
            mixed = jnp.dot(ws_ref[g], vn[rows, :], preferred_element_type=F32) + bsb_ref[g]
            ob_ref[rows, sl] = u[rows, :] * mixed


def _row_seg(i, n_lat_tiles, tiles_per_seq, n_batch):
    return jnp.where(i < n_lat_tiles, i // tiles_per_seq, n_batch)


def _inproj(xs, modt, norm_g, w_in_bf, rope, qn_a, kn_a, qn_c, kn_c, vn_b, ws_bf, bsb, dims):
    n_batch, seq, ctx_len, d = dims
    t_rows = xs.shape[0]
    tm = _pick_tile(512, seq, n_batch * ctx_len)
    n_lat_tiles = n_batch * seq // tm
    tps = seq // tm
    cos_t, s1_t, s2_t = rope(tm)

    def mod_spec(k):
        return pl.BlockSpec((1, 1, d), lambda i: (_row_seg(i, n_lat_tiles, tps, n_batch) * N_MOD + k, 0, 0))

    def rope_spec():
        return pl.BlockSpec((tm, HEAD_DIM), lambda i: (jnp.where(i < n_lat_tiles, i % tps, tps), 0))

    def full(shape):
        return pl.BlockSpec(shape, lambda i: (0,) * len(shape))

    def out(width, dtype):
        return pl.BlockSpec((tm, width), lambda i: (i, 0)), jax.ShapeDtypeStruct((t_rows, width), dtype)

    outs = [out(A_WIDTH, BF16), out(KV_WIDTH, BF16), out(KV_WIDTH, BF16),
            out(C_WIDTH, BF16), out(KV_WIDTH, BF16), out(KV_WIDTH, BF16), out(B_WIDTH, F32)]
    return pl.pallas_call(
        functools.partial(_inproj_kernel, tm=tm),
        grid=(t_rows // tm,),
        in_specs=[
            pl.BlockSpec((tm, d), lambda i: (i, 0)),
            mod_spec(0), mod_spec(1),
            full((1, d)),
            full((d, IN_WIDTH)),
            rope_spec(), rope_spec(), rope_spec(),
            full((1, HEAD_DIM)), full((1, HEAD_DIM)), full((1, HEAD_DIM)), full((1, HEAD_DIM)),
            full((1, B_WIDTH)),
            full((B_GROUPS, B_CHUNK, B_CHUNK)),
            full((B_GROUPS, B_CHUNK, HEAD_DIM)),
        ],
        out_specs=[o[0] for o in outs],
        out_shape=[o[1] for o in outs],
        compiler_params=_params(("arbitrary",)),
        name="inproj",
    )(xs, modt, modt, norm_g, w_in_bf, cos_t, s1_t, s2_t, qn_a, kn_a, qn_c, kn_c, vn_b, ws_bf, bsb)


def _softmax_pv(s_list, v_list, sink):
    m = s_list[0].max(axis=-1, keepdims=True)
    for s in s_list[1:]:
        m = jnp.maximum(m, s.max(axis=-1, keepdims=True))
    if sink is not None:
        m = jnp.maximum(m, sink)
    l = None
    o = None
    for s, v in zip(s_list, v_list):
        p = jnp.exp(s - m)
        ls = jnp.sum(p, axis=-1, keepdims=True)
        os_ = jnp.dot(p.astype(BF16), v, preferred_element_type=F32)
        l = ls if l is None else l + ls
        o = os_ if o is None else o + os_
    if sink is not None:
        l = l + jnp.exp(sink - m)
    return o / l


def _attn_win_kernel(sink_ref, q_ref, kx_ref, kp_ref, km_ref, kn_ref, vx_ref, vp_ref, vm_ref, vn_ref,
                     o_ref, *, tq, seq, n_g):
    i = pl.program_id(1)
    kvh = pl.program_id(2)
    ctx_len = kx_ref.shape[0]
    kband = jnp.concatenate([kp_ref[...], km_ref[...], kn_ref[...]], axis=0)
    vband = jnp.concatenate([vp_ref[...], vm_ref[...], vn_ref[...]], axis=0)
    nb = tq + 2 * WINDOW
    r = lax.broadcasted_iota(I32, (tq, nb), 0)
    c = lax.broadcasted_iota(I32, (tq, nb), 1)
    kpos = c - WINDOW + i * tq
    qpos = r + i * tq
    valid = (jnp.abs(kpos - qpos) <= WINDOW) & (kpos >= 0) & (kpos < seq)
    kx, vx = kx_ref[...], vx_ref[...]
    for g in range(n_g):
        sl = slice(g * HEAD_DIM, (g + 1) * HEAD_DIM)
        q = q_ref[:, sl]
        s_ctx = _dot_nt(q, kx)
        s_band = jnp.where(valid, _dot_nt(q, kband), NEG_INF)
        o_ref[:, sl] = _softmax_pv([s_ctx, s_band], [vx, vband], sink_ref[kvh * n_g + g])
    del ctx_len


def _attn_win(sink, qa, ka, va, dims):
    n_batch, seq, ctx_len, _ = dims
    t_rows = n_batch * seq
    tq = _pick_tile(512, seq)
    nq = seq // tq
    n_g = A_HEADS // A_KV
    wb = tq // WINDOW
    sb = seq // WINDOW
    lat_ctx_blocks = n_batch * seq // ctx_len

    def q_map(b, i, k):
        return (b * nq + i, k)

    def prev_map(b, i, k):
        return (b * sb + jnp.maximum(i * wb - 1, 0), k)

    def next_map(b, i, k):
        return (b * sb + jnp.minimum((i + 1) * wb, sb - 1), k)

    def ctx_map(b, i, k):
        return (lat_ctx_blocks + b, k)

    kv_specs = [pl.BlockSpec((ctx_len, HEAD_DIM), ctx_map), pl.BlockSpec((WINDOW, HEAD_DIM), prev_map),
                pl.BlockSpec((tq, HEAD_DIM), q_map), pl.BlockSpec((WINDOW, HEAD_DIM), next_map)]
    return pl.pallas_call(
        functools.partial(_attn_win_kernel, tq=tq, seq=seq, n_g=n_g),
        grid=(n_batch, nq, A_KV),
        in_specs=[pl.BlockSpec(memory_space=pltpu.SMEM),
                  pl.BlockSpec((tq, n_g * HEAD_DIM), q_map)] + kv_specs + kv_specs,
        out_specs=pl.BlockSpec((tq, n_g * HEAD_DIM), q_map),
        out_shape=jax.ShapeDtypeStruct((t_rows, A_WIDTH), F32),
        compiler_params=_params(("arbitrary",) * 3),
        name="attn_window",
    )(sink, qa, ka, ka, ka, ka, va, va, va, va)


def _attn_glob_kernel(q_ref, kx_ref, kl_ref, vx_ref, vl_ref, o_ref, *, n_g):
    kx, kl, vx, vl = kx_ref[...], kl_ref[...], vx_ref[...], vl_ref[...]
    for g in range(n_g):
        sl = slice(g * HEAD_DIM, (g + 1) * HEAD_DIM)
        q = q_ref[:, sl]
        o_ref[:, sl] = _softmax_pv([_dot_nt(q, kx), _dot_nt(q, kl)], [vx, vl], None)


def _attn_glob(qc, kc, vc, dims):
    n_batch, seq, ctx_len, _ = dims
    t_rows = n_batch * seq
    tq = _pick_tile(256, seq)
    nq = seq // tq
    n_g = C_HEADS // C_KV
    lat_ctx_blocks = n_batch * seq // ctx_len

    def q_map(b, k, i):
        return (b * nq + i, k)

    kv_specs = [pl.BlockSpec((ctx_len, HEAD_DIM), lambda b, k, i: (lat_ctx_blocks + b, k)),
                pl.BlockSpec((seq, HEAD_DIM), lambda b, k, i: (b, k))]
    return pl.pallas_call(
        functools.partial(_attn_glob_kernel, n_g=n_g),
        grid=(n_batch, C_KV, nq),
        in_specs=[pl.BlockSpec((tq, n_g * HEAD_DIM), q_map)] + kv_specs + kv_specs,
        out_specs=pl.BlockSpec((tq, n_g * HEAD_DIM), q_map),
        out_shape=jax.ShapeDtypeStruct((t_rows, C_WIDTH), F32),
        compiler_params=_params(("arbitrary",) * 3),
        name="attn_global",
    )(qc, kc, kc, vc, vc)


def _attn_ctx_kernel(sink_ref, q_ref, k_ref, v_ref, o_ref, *, n_g, use_sink):
    kvh = pl.program_id(1)
    k, v = k_ref[...], v_ref[...]
    for g in range(n_g):
        sl = slice(g * HEAD_DIM, (g + 1) * HEAD_DIM)
        sink = sink_ref[kvh * n_g + g] if use_sink else None
        o_ref[:, sl] = _softmax_pv([_dot_nt(q_ref[:, sl], k)], [v], sink)


def _attn_ctx(sink, q, k, v, n_heads, n_kv, use_sink, dims):
    n_batch, seq, ctx_len, _ = dims
    n_g = n_heads // n_kv
    lat_ctx_blocks = n_batch * seq // ctx_len

    def blk(b, k):
        return (lat_ctx_blocks + b, k)

    return pl.pallas_call(
        functools.partial(_attn_ctx_kernel, n_g=n_g, use_sink=use_sink),
        grid=(n_batch, n_kv),
        in_specs=[pl.BlockSpec(memory_space=pltpu.SMEM),
                  pl.BlockSpec((ctx_len, n_g * HEAD_DIM), blk),
                  pl.BlockSpec((ctx_len, HEAD_DIM), blk),
                  pl.BlockSpec((ctx_len, HEAD_DIM), blk)],
        out_specs=pl.BlockSpec((ctx_len, n_g * HEAD_DIM), lambda b, k: (b, k)),
        out_shape=jax.ShapeDtypeStruct((n_batch * ctx_len, n_heads * HEAD_DIM), F32),
        compiler_params=_params(("arbitrary",) * 2),
        name="attn_context",
    )(sink, q, k, v)


def _merge_kernel(*refs, n_lat_tiles, has_ctx):
    if has_ctx:
        oa_ref, oc_ref, oax_ref, ocx_ref = refs[:4]
        refs = refs[4:]
    else:
        oa_ref, oc_ref = refs[:2]
        refs = refs[2:]
    ob_ref, og_ref, w_ref, x_ref, g1_ref, sh_ref, sc_ref, n2_ref, wr_ref, xo_ref, h_ref, aff_ref = refs
    is_lat = pl.program_id(0) < n_lat_tiles

    def group(o, off, width):
        y = (_rms_scale(o) * og_ref[:, off:off + width]).astype(BF16)
        return jnp.dot(y, w_ref[off:off + width, :], preferred_element_type=F32)

    o_a, o_c = oa_ref[...], oc_ref[...]
    if has_ctx:
        o_a = jnp.where(is_lat, o_a, oax_ref[...])
        o_c = jnp.where(is_lat, o_c, ocx_ref[...])
    y = group(o_a, 0, A_WIDTH) + group(ob_ref[...], A_WIDTH, B_WIDTH) + group(o_c, A_WIDTH + B_WIDTH, C_WIDTH)
    xn = x_ref[...] + g1_ref[0] * y
    xo_ref[...] = xn
    h = _rms_scale(xn) * n2_ref[...]
    h = h * (1.0 + sc_ref[0]) + sh_ref[0]
    h_ref[...] = h
    logits = jnp.dot(h, wr_ref[...], precision=lax.Precision.HIGHEST, preferred_element_type=F32)
    e = jnp.exp(logits - logits.max(axis=-1, keepdims=True))
    aff_ref[...] = e / jnp.sum(e, axis=-1, keepdims=True)


def _merge(o_a, o_c, o_ax, o_cx, o_b, out_g, w_out_bf, xs, modt, norm2_g, w_router, rows_out, dims):
    n_batch, seq, ctx_len, d = dims
    tm = _pick_tile(512, seq, n_batch * ctx_len)
    n_lat_tiles = n_batch * seq // tm
    tps = seq // tm
    n_exp = w_router.shape[1]
    mix = A_WIDTH + B_WIDTH + C_WIDTH
    has_ctx = o_ax is not None
    assert has_ctx == (rows_out > n_batch * seq)

    def mod_spec(k):
        return pl.BlockSpec((1, 1, d), lambda i: (_row_seg(i, n_lat_tiles, tps, n_batch) * N_MOD + k, 0, 0))

    def rows(width):
        return pl.BlockSpec((tm, width), lambda i: (i, 0))

    def lat_rows(width):
        return pl.BlockSpec((tm, width), lambda i: (jnp.minimum(i, n_lat_tiles - 1), 0))

    def ctx_rows(width):
        return pl.BlockSpec((tm, width), lambda i: (jnp.maximum(i - n_lat_tiles, 0), 0))

    def full(shape):
        return pl.BlockSpec(shape, lambda i: (0,) * len(shape))

    attn_specs = [lat_rows(A_WIDTH), lat_rows(C_WIDTH)]
    attn_args = [o_a, o_c]
    if has_ctx:
        attn_specs += [ctx_rows(A_WIDTH), ctx_rows(C_WIDTH)]
        attn_args += [o_ax, o_cx]
    return pl.pallas_call(
        functools.partial(_merge_kernel, n_lat_tiles=n_lat_tiles, has_ctx=has_ctx),
        grid=(rows_out // tm,),
        in_specs=attn_specs + [rows(B_WIDTH), full((1, mix)), full((mix, d)), rows(d),
                               mod_spec(2), mod_spec(3), mod_spec(4), full((1, d)), full((d, n_exp))],
        out_specs=[rows(d), rows(d), rows(n_exp)],
        out_shape=[jax.ShapeDtypeStruct((rows_out, d), F32), jax.ShapeDtypeStruct((rows_out, d), F32),
                   jax.ShapeDtypeStruct((rows_out, n_exp), F32)],
        compiler_params=_params(("arbitrary",)),
        name="merge_router",
    )(*attn_args, o_b, out_g, w_out_bf, xs, modt, modt, modt, norm2_g, w_router)


def _prefix_excl(mask_f32, tri):
    rows, n = mask_f32.shape
    cw = tri.shape[0]
    carry = jnp.zeros((rows, 1), F32)
    parts = []
    for c in range(n // cw):
        m = mask_f32[:, c * cw:(c + 1) * cw]
        parts.append(jnp.dot(m.astype(BF16), tri, preferred_element_type=F32) + carry)
        carry = carry + jnp.sum(m, axis=-1, keepdims=True)
    return parts[0] if len(parts) == 1 else jnp.concatenate(parts, axis=-1)


def _route_kernel(aff_ref, idx_ref, gate_ref, *, n, cap, n_exp):
    a = aff_ref[0]
    bits = pltpu.bitcast(a, I32)

    def bit_step(b, thr):
        cand = thr | jnp.left_shift(jnp.int32(1), 30 - b)
        cnt = jnp.sum(jnp.where(bits >= cand, 1.0, 0.0), axis=-1, keepdims=True)
        return jnp.where(cnt >= cap, cand, thr)

    thr = lax.fori_loop(0, 31, bit_step, jnp.zeros((n_exp, 1), I32))
    gt = bits > thr
    eq = bits == thr
    cw = min(256, n)
    tri = jnp.where(lax.broadcasted_iota(I32, (cw, cw), 0) < lax.broadcasted_iota(I32, (cw, cw), 1),
                    1.0, 0.0).astype(BF16)
    need = cap - jnp.sum(jnp.where(gt, 1.0, 0.0), axis=-1, keepdims=True)
    sel = gt | (eq & (_prefix_excl(jnp.where(eq, 1.0, 0.0), tri) < need))
    pos = jnp.where(sel, _prefix_excl(jnp.where(sel, 1.0, 0.0), tri), -1.0)

    tok = lax.broadcasted_iota(I32, (1, n), 1)
    t_hi = (tok // 64).astype(F32)
    t_lo = (tok % 64).astype(F32)
    g_hi = a.astype(BF16).astype(F32)
    r1 = a - g_hi
    g_mid = r1.astype(BF16).astype(F32)
    g_lo = r1 - g_mid
    slot = lax.broadcasted_iota(I32, (cap, n), 0).astype(F32)
    row = lax.broadcasted_iota(I32, (16, n), 0)
    for e in range(n_exp):
        onehot = jnp.where(pos[e:e + 1, :] == slot, 1.0, 0.0).astype(BF16)
        lhs = jnp.where(row == 0, t_hi,
              jnp.where(row == 1, t_lo,
              jnp.where(row == 2, g_hi[e:e + 1, :],
              jnp.where(row == 3, g_mid[e:e + 1, :],
              jnp.where(row == 4, g_lo[e:e + 1, :], 0.0))))).astype(BF16)
        res = _dot_nt(lhs, onehot)
        idx_ref[0, e:e + 1, :] = (res[0:1, :] * 64.0 + res[1:2, :]).astype(I32)
        gate_ref[0, e:e + 1, :] = res[2:3, :] + res[3:4, :] + res[4:5, :]


def _route(aff_sets):
    n_sets, n_exp, n = aff_sets.shape
    cap = EC_CAPACITY * n // n_exp
    spec = pl.BlockSpec((1, n_exp, cap), lambda s: (s, 0, 0))
    return pl.pallas_call(
        functools.partial(_route_kernel, n=n, cap=cap, n_exp=n_exp),
        grid=(n_sets,),
        in_specs=[pl.BlockSpec((1, n_exp, n), lambda s: (s, 0, 0))],
        out_specs=[spec, spec],
        out_shape=[jax.ShapeDtypeStruct((n_sets, n_exp, cap), I32),
                   jax.ShapeDtypeStruct((n_sets, n_exp, cap), F32)],
        compiler_params=_params(("arbitrary",)),
        name="route",
    )(aff_sets)


def _ffn_kernel(idx_ref, h_hbm, gate_ref, g2_ref, wg_ref, wu_ref, wd_ref, x_in_hbm, x_hbm,
                buf, hb, yacc, sem_h, sem_x, sem_w, *, m, groups, n_split, per, ctx_seg):
    del x_in_hbm
    e = pl.program_id(0)
    part = pl.program_id(1)
    f = pl.program_id(2)
    n_f = pl.num_programs(2)
    base = (e * n_split + part) * m

    def row_copy(src, j, sem, to_vmem):
        r = idx_ref[base + j]
        if to_vmem:
            return pltpu.make_async_copy(src.at[pl.ds(r, 1), :], buf.at[pl.ds(j, 1), :], sem)
        return pltpu.make_async_copy(buf.at[pl.ds(j, 1), :], src.at[pl.ds(r, 1), :], sem)

    def for_rows(fn):
        def body(j, carry):
            fn(j)
            return carry
        lax.fori_loop(0, m, body, 0)

    @pl.when(f == 0)
    def _():
        for_rows(lambda j: row_copy(h_hbm, j, sem_h, True).start())
        for_rows(lambda j: row_copy(h_hbm, j, sem_h, True).wait())
        hb[...] = buf[...].astype(BF16)
        for_rows(lambda j: row_copy(x_hbm, j, sem_x, True).start())
        yacc[...] = jnp.zeros_like(yacc)

    h = hb[...]
    a = jnp.dot(h, wg_ref[...].astype(BF16), preferred_element_type=F32)
    u = jnp.dot(h, wu_ref[...].astype(BF16), preferred_element_type=F32)
    z = (a / (1.0 + jnp.exp(-a)) * u).astype(BF16)
    yacc[...] += jnp.dot(z, wd_ref[...].astype(BF16), preferred_element_type=F32)

    @pl.when(f == n_f - 1)
    def _():
        for_rows(lambda j: row_copy(x_hbm, j, sem_x, True).wait())
        for off, size, sample in groups:
            seg = ctx_seg if sample is None else part * per + sample
            rows = pl.ds(off, size)
            buf[rows, :] = buf[rows, :] + g2_ref[pl.ds(seg, 1), :] * (gate_ref[0, 0, rows, :] * yacc[rows, :])
        for_rows(lambda j: row_copy(x_hbm, j, sem_w, False).start())
        for_rows(lambda j: row_copy(x_hbm, j, sem_w, False).wait())


def _experts(idx_flat, gate, h2, g2tab, w_gate, w_up, w_down, xs, layer, m, groups, n_split, per, ctx_seg):
    d = xs.shape[1]
    n_exp, d_exp = w_gate.shape[1], w_gate.shape[3]
    tf = _pick_tile(256, d_exp)
    grid_spec = pltpu.PrefetchScalarGridSpec(
        num_scalar_prefetch=1,
        grid=(n_exp, n_split, d_exp // tf),
        in_specs=[
            pl.BlockSpec(memory_space=pl.ANY),
            pl.BlockSpec((1, 1, m, 1), lambda e, p, f, idx: (e, p, 0, 0)),
            pl.BlockSpec(g2tab.shape, lambda e, p, f, idx: (0, 0)),
            pl.BlockSpec((None, None, d, tf), lambda e, p, f, idx: (layer, e, 0, f)),
            pl.BlockSpec((None, None, d, tf), lambda e, p, f, idx: (layer, e, 0, f)),
            pl.BlockSpec((None, None, tf, d), lambda e, p, f, idx: (layer, e, f, 0)),
            pl.BlockSpec(memory_space=pl.ANY),
        ],
        out_specs=pl.BlockSpec(memory_space=pl.ANY),
        scratch_shapes=[pltpu.VMEM((m, d), F32), pltpu.VMEM((m, d), BF16), pltpu.VMEM((m, d), F32),
                        pltpu.SemaphoreType.DMA, pltpu.SemaphoreType.DMA, pltpu.SemaphoreType.DMA],
    )
    return pl.pallas_call(
        functools.partial(_ffn_kernel, m=m, groups=tuple(groups), n_split=n_split, per=per, ctx_seg=ctx_seg),
        grid_spec=grid_spec,
        out_shape=jax.ShapeDtypeStruct(xs.shape, xs.dtype),
        input_output_aliases={7: 0},
        compiler_params=_params(("arbitrary",) * 3),
        name="experts",
    )(idx_flat, h2, gate, g2tab, w_gate, w_up, w_down, xs)


def _rope_tables(seq):
    pos = np.arange(seq)
    n_freq = HEAD_DIM // 4
    inv = jnp.asarray(ROPE_THETA, F32) ** (-jnp.arange(n_freq, dtype=F32) / n_freq)
    row = jnp.asarray(pos // GRID_W, F32)[:, None] * inv
    col = jnp.asarray(pos % GRID_W, F32)[:, None] * inv
    zero = jnp.zeros_like(row)
    cos = jnp.concatenate([jnp.cos(row), jnp.cos(row), jnp.cos(col), jnp.cos(col)], axis=-1)
    s1 = jnp.concatenate([zero, jnp.sin(row), zero, jnp.sin(col)], axis=-1)
    s2 = jnp.concatenate([-jnp.sin(row), zero, -jnp.sin(col), zero], axis=-1)

    def padded(tm):
        ident = jnp.ones((tm, HEAD_DIM), F32)
        none = jnp.zeros((tm, HEAD_DIM), F32)
        return (jnp.concatenate([cos, ident], axis=0), jnp.concatenate([s1, none], axis=0),
                jnp.concatenate([s2, none], axis=0))

    return padded


def _expert_rows(idx_l, gate_l, idx_c, gate_c, dims, n_split):
    n_batch, seq, ctx_len, _ = dims
    per = n_batch // n_split
    n_exp = idx_l.shape[1]

    def arrange(v):
        s = v.shape[-1]
        return v.reshape(n_split, per, n_exp, s).transpose(2, 0, 1, 3).reshape(n_exp, n_split, per * s)

    rows_l = idx_l + (jnp.arange(n_batch, dtype=I32) * seq)[:, None, None]
    parts_i, parts_g = [arrange(rows_l)], [arrange(gate_l)]
    groups = [(b * idx_l.shape[-1], idx_l.shape[-1], b) for b in range(per)]
    if idx_c is not None:
        rows_c = idx_c + (n_batch * seq + jnp.arange(n_batch, dtype=I32) * ctx_len)[:, None, None]
        parts_i.append(arrange(rows_c))
        parts_g.append(arrange(gate_c))
        groups += [(per * idx_l.shape[-1], per * idx_c.shape[-1], None)]
    idx = jnp.concatenate(parts_i, axis=-1)
    gate = jnp.concatenate(parts_g, axis=-1)
    return idx, gate, groups


def kernel(x, c, ctx, c_ctx, w_mod, b_mod, norm1_g, norm2_g, w_in, qn_a, kn_a, sink_a, vn_b, w_s, b_s,
           qn_c, kn_c, out_g, w_out, w_router, w_gate, w_up, w_down):
    n_batch, seq, d = x.shape
    ctx_len = ctx.shape[1]
    depth = w_mod.shape[0]
    n_exp = w_router.shape[2]
    dims = (n_batch, seq, ctx_len, d)
    t_lat = n_batch * seq
    assert n_batch < MOD_ROWS and seq % WINDOW == 0 and ctx_len % B_CHUNK == 0 and t_lat % ctx_len == 0
    n_split = 2 if n_batch % 2 == 0 else 1
    per = n_batch // n_split

    xs = jnp.concatenate([x.reshape(t_lat, d), ctx.reshape(n_batch * ctx_len, d)], axis=0)
    cvec = jnp.zeros((MOD_ROWS, d), F32).at[:n_batch].set(c).at[n_batch].set(c_ctx)
    rope = _rope_tables(seq)

    for layer in range(depth):
        last = layer == depth - 1
        mod = _modulation(cvec, w_mod, b_mod, layer)
        modt = mod.reshape(MOD_ROWS * N_MOD, 1, d)
        g2tab = mod.reshape(MOD_ROWS, N_MOD, d)[:, N_MOD - 1, :]
        row = lambda v: v[layer].reshape(1, -1)
        bsb = jnp.broadcast_to(b_s[layer][:, :, None], (B_GROUPS, B_CHUNK, HEAD_DIM))
        qa, ka, va, qc, kc, vc, o_b = _inproj(
            xs, modt, row(norm1_g), w_in[layer].astype(BF16), rope, row(qn_a), row(kn_a), row(qn_c),
            row(kn_c), row(vn_b), w_s[layer].astype(BF16), bsb, dims)
        o_a = _attn_win(sink_a[layer], qa, ka, va, dims)
        o_c = _attn_glob(qc, kc, vc, dims)
        o_ax = o_cx = None
        if not last:
            o_ax = _attn_ctx(sink_a[layer], qa, ka, va, A_HEADS, A_KV, True, dims)
            o_cx = _attn_ctx(sink_a[layer], qc, kc, vc, C_HEADS, C_KV, False, dims)
        rows_out = t_lat if last else xs.shape[0]
        xs, h2, aff = _merge(o_a, o_c, o_ax, o_cx, o_b, row(out_g), w_out[layer].astype(BF16), xs, modt,
                             row(norm2_g), w_router[layer], rows_out, dims)

        idx_l, gate_l = _route(aff[:t_lat].reshape(n_batch, seq, n_exp).transpose(0, 2, 1))
        idx_c = gate_c = None
        if not last:
            idx_c, gate_c = _route(aff[t_lat:].reshape(n_batch, ctx_len, n_exp).transpose(0, 2, 1))
        idx, gate, groups = _expert_rows(idx_l, gate_l, idx_c, gate_c, dims, n_split)
        m = idx.shape[-1]
        xs = _experts(idx.reshape(-1), gate.reshape(n_exp, n_split, m, 1), h2, g2tab, w_gate, w_up, w_down,
                      xs, layer, m, groups, n_split, per, n_batch)
    return xs.reshape(n_batch, seq, d)
```

```python
import functools

import jax
import jax.numpy as jnp
import numpy as np
from jax import lax
from jax.experimental import pallas as pl
from jax.experimental.pallas import tpu as pltpu

F32 = jnp.float32
BF16 = jnp.bfloat16
I32 = jnp.int32
U32 = jnp.uint32

HEAD_DIM = 128
GRID_W = 64
WINDOW = 128
A_HEADS, A_KV = 6, 2
B_GROUPS, B_CHUNK = 4, 128
C_HEADS, C_KV = 6, 2
A_WIDTH = A_HEADS * HEAD_DIM
B_WIDTH = B_GROUPS * HEAD_DIM
C_WIDTH = C_HEADS * HEAD_DIM
KV_WIDTH = A_KV * HEAD_DIM
OFF_QA = 0
OFF_KA = OFF_QA + A_WIDTH
OFF_VA = OFF_KA + KV_WIDTH
OFF_UB = OFF_VA + KV_WIDTH
OFF_VB = OFF_UB + B_WIDTH
OFF_QC = OFF_VB + B_WIDTH
OFF_KC = OFF_QC + C_WIDTH
OFF_VC = OFF_KC + KV_WIDTH
IN_WIDTH = OFF_VC + KV_WIDTH
EC_CAPACITY = 2
N_MOD = 6
ROPE_THETA = 10000.0
EPS = 1e-6
NEG_INF = -1e30
QK_SCALE = HEAD_DIM ** -0.5
MOD_ROWS = 8

VMEM_LIMIT = 56 * 1024 * 1024


def _params(sem, vmem=VMEM_LIMIT):
    return pltpu.CompilerParams(dimension_semantics=sem, vmem_limit_bytes=vmem)


def _pick_tile(target, *sizes):
    t = target
    while any(s % t for s in sizes):
        t //= 2
    return t


def _rms_scale(x):
    return x * lax.rsqrt(jnp.mean(x * x, axis=-1, keepdims=True) + EPS)


def _dot_nt(a, b):
    return lax.dot_general(a, b, (((1,), (1,)), ((), ())), preferred_element_type=F32)


def _mod_kernel(c_ref, w_ref, b_ref, o_ref):
    c = c_ref[...]
    s = c / (1.0 + jnp.exp(-c))
    o_ref[...] = jnp.dot(s, w_ref[...], precision=lax.Precision.HIGHEST,
                         preferred_element_type=F32) + b_ref[...]


def _modulation(cvec, w_mod, b_mod, layer):
    d = cvec.shape[1]
    n = w_mod.shape[2]
    tn = _pick_tile(1024, n)
    return pl.pallas_call(
        _mod_kernel,
        grid=(n // tn,),
        in_specs=[
            pl.BlockSpec((MOD_ROWS, d), lambda j: (0, 0)),
            pl.BlockSpec((None, d, tn), lambda j: (layer, 0, j)),
            pl.BlockSpec((None, 1, tn), lambda j: (layer, 0, j)),
        ],
        out_specs=pl.BlockSpec((MOD_ROWS, tn), lambda j: (0, j)),
        out_shape=jax.ShapeDtypeStruct((MOD_ROWS, n), F32),
        compiler_params=_params(("arbitrary",)),
        name="modulation",
    )(cvec, w_mod, b_mod.reshape(b_mod.shape[0], 1, n))


def _inproj_kernel(x_ref, sh_ref, sc_ref, g_ref, w_ref, cos_ref, s1_ref, s2_ref,
                   qna_ref, kna_ref, qnc_ref, knc_ref, vnb_ref, ws_ref, bsb_ref,
                   qa_ref, ka_ref, va_ref, qc_ref, kc_ref, vc_ref, ob_ref, *, tm):
    x = x_ref[...]
    h = _rms_scale(x) * g_ref[...]
    h = h * (1.0 + sc_ref[0]) + sh_ref[0]
    hb = h.astype(BF16)
    cos, s1, s2 = cos_ref[...], s1_ref[...], s2_ref[...]

    def proj(off, width):
        return jnp.dot(hb, w_ref[:, off:off + width], preferred_element_type=F32)

    def qk_head(p, gain, scale):
        q = _rms_scale(p) * gain
        q = q * cos + pltpu.roll(q, 32, 1) * s1 + pltpu.roll(q, 96, 1) * s2
        if scale != 1.0:
            q = q * scale
        return q.astype(BF16)

    def heads(off, n_heads, gain_ref, scale, out_ref):
        p = proj(off, n_heads * HEAD_DIM)
        for hd in range(n_heads):
            sl = slice(hd * HEAD_DIM, (hd + 1) * HEAD_DIM)
            out_ref[:, sl] = qk_head(p[:, sl], gain_ref[...], scale)

    heads(OFF_QA, A_HEADS, qna_ref, QK_SCALE, qa_ref)
    heads(OFF_KA, A_KV, kna_ref, 1.0, ka_ref)
    va_ref[...] = proj(OFF_VA, KV_WIDTH).astype(BF16)
    heads(OFF_QC, C_HEADS, qnc_ref, QK_SCALE, qc_ref)
    heads(OFF_KC, C_KV, knc_ref, 1.0, kc_ref)
    vc_ref[...] = proj(OFF_VC, KV_WIDTH).astype(BF16)

    pu = proj(OFF_UB, B_WIDTH)
    pv = proj(OFF_VB, B_WIDTH)
    for g in range(B_GROUPS):
        sl = slice(g * HEAD_DIM, (g + 1) * HEAD_DIM)
        u = jax.nn.gelu(pu[:, sl])
        vn = (_rms_scale(jax.nn.gelu(pv[:, sl])) * vnb_ref[:, sl]).astype(BF16)
        for c in range(tm // B_CHUNK):
            rows = slice(c * B_CHUNK, (c + 1) * B_CHUNK)
            mixed = jnp.dot(ws_ref[g], vn[rows, :], preferred_element_type=F32) + bsb_ref[g]
            ob_ref[rows, sl] = u[rows, :] * mixed


def _row_seg(i, n_lat_tiles, tiles_per_seq, n_batch):
    return jnp.where(i < n_lat_tiles, i // tiles_per_seq, n_batch)


def _inproj(xs, modt, norm_g, w_in_bf, rope, qn_a, kn_a, qn_c, kn_c, vn_b, ws_bf, bsb, dims):
    n_batch, seq, ctx_len, d = dims
    t_rows = xs.shape[0]
    tm = _pick_tile(512, seq, n_batch * ctx_len)
    n_lat_tiles = n_batch * seq // tm
    tps = seq // tm
    cos_t, s1_t, s2_t = rope(tm)

    def mod_spec(k):
        return pl.BlockSpec((1, 1, d), lambda i: (_row_seg(i, n_lat_tiles, tps, n_batch) * N_MOD + k, 0, 0))

    def rope_spec():
        return pl.BlockSpec((tm, HEAD_DIM), lambda i: (jnp.where(i < n_lat_tiles, i % tps, tps), 0))

    def full(shape):
        return pl.BlockSpec(shape, lambda i: (0,) * len(shape))

    def out(width, dtype):
        return pl.BlockSpec((tm, width), lambda i: (i, 0)), jax.ShapeDtypeStruct((t_rows, width), dtype)

    outs = [out(A_WIDTH, BF16), out(KV_WIDTH, BF16), out(KV_WIDTH, BF16),
            out(C_WIDTH, BF16), out(KV_WIDTH, BF16), out(KV_WIDTH, BF16), out(B_WIDTH, F32)]
    return pl.pallas_call(
        functools.partial(_inproj_kernel, tm=tm),
        grid=(t_rows // tm,),
        in_specs=[
            pl.BlockSpec((tm, d), lambda i: (i, 0)),
            mod_spec(0), mod_spec(1),
            full((1, d)),
            full((d, IN_WIDTH)),
            rope_spec(), rope_spec(), rope_spec(),
            full((1, HEAD_DIM)), full((1, HEAD_DIM)), full((1, HEAD_DIM)), full((1, HEAD_DIM)),
            full((1, B_WIDTH)),
            full((B_GROUPS, B_CHUNK, B_CHUNK)),
            full((B_GROUPS, B_CHUNK, HEAD_DIM)),
        ],
        out_specs=[o[0] for o in outs],
        out_shape=[o[1] for o in outs],
        compiler_params=_params(("arbitrary",)),
        name="inproj",
    )(xs, modt, modt, norm_g, w_in_bf, cos_t, s1_t, s2_t, qn_a, kn_a, qn_c, kn_c, vn_b, ws_bf, bsb)


def _softmax_pv(s_list, v_list, sink):
    m = s_list[0].max(axis=-1, keepdims=True)
    for s in s_list[1:]:
        m = jnp.maximum(m, s.max(axis=-1, keepdims=True))
    if sink is not None:
        m = jnp.maximum(m, sink)
    l = None
    o = None
    for s, v in zip(s_list, v_list):
        p = jnp.exp(s - m)
        ls = jnp.sum(p, axis=-1, keepdims=True)
        os_ = jnp.dot(p.astype(BF16), v, preferred_element_type=F32)
        l = ls if l is None else l + ls
        o = os_ if o is None else o + os_
    if sink is not None:
        l = l + jnp.exp(sink - m)
    return o / l


def _attn_win_kernel(sink_ref, q_ref, kx_ref, kp_ref, km_ref, kn_ref, vx_ref, vp_ref, vm_ref, vn_ref,
                     o_ref, *, tq, seq, n_g):
    i = pl.program_id(1)
    kvh = pl.program_id(2)
    ctx_len = kx_ref.shape[0]
    kband = jnp.concatenate([kp_ref[...], km_ref[...], kn_ref[...]], axis=0)
    vband = jnp.concatenate([vp_ref[...], vm_ref[...], vn_ref[...]], axis=0)
    nb = tq + 2 * WINDOW
    r = lax.broadcasted_iota(I32, (tq, nb), 0)
    c = lax.broadcasted_iota(I32, (tq, nb), 1)
    kpos = c - WINDOW + i * tq
    qpos = r + i * tq
    valid = (jnp.abs(kpos - qpos) <= WINDOW) & (kpos >= 0) & (kpos < seq)
    kx, vx = kx_ref[...], vx_ref[...]
    for g in range(n_g):
        sl = slice(g * HEAD_DIM, (g + 1) * HEAD_DIM)
        q = q_ref[:, sl]
        s_ctx = _dot_nt(q, kx)
        s_band = jnp.where(valid, _dot_nt(q, kband), NEG_INF)
        o_ref[:, sl] = _softmax_pv([s_ctx, s_band], [vx, vband], sink_ref[kvh * n_g + g])
    del ctx_len


def _attn_win(sink, qa, ka, va, dims):
    n_batch, seq, ctx_len, _ = dims
    t_rows = n_batch * seq
    tq = _pick_tile(512, seq)
    nq = seq // tq
    n_g = A_HEADS // A_KV
    wb = tq // WINDOW
    sb = seq // WINDOW
    lat_ctx_blocks = n_batch * seq // ctx_len

    def q_map(b, i, k):
        return (b * nq + i, k)

    def prev_map(b, i, k):
        return (b * sb + jnp.maximum(i * wb - 1, 0), k)

    def next_map(b, i, k):
        return (b * sb + jnp.minimum((i + 1) * wb, sb - 1), k)

    def ctx_map(b, i, k):
        return (lat_ctx_blocks + b, k)

    kv_specs = [pl.BlockSpec((ctx_len, HEAD_DIM), ctx_map), pl.BlockSpec((WINDOW, HEAD_DIM), prev_map),
                pl.BlockSpec((tq, HEAD_DIM), q_map), pl.BlockSpec((WINDOW, HEAD_DIM), next_map)]
    return pl.pallas_call(
        functools.partial(_attn_win_kernel, tq=tq, seq=seq, n_g=n_g),
        grid=(n_batch, nq, A_KV),
        in_specs=[pl.BlockSpec(memory_space=pltpu.SMEM),
                  pl.BlockSpec((tq, n_g * HEAD_DIM), q_map)] + kv_specs + kv_specs,
        out_specs=pl.BlockSpec((tq, n_g * HEAD_DIM), q_map),
        out_shape=jax.ShapeDtypeStruct((t_rows, A_WIDTH), F32),
        compiler_params=_params(("arbitrary",) * 3),
        name="attn_window",
    )(sink, qa, ka, ka, ka, ka, va, va, va, va)


def _attn_glob_kernel(q_ref, kx_ref, kl_ref, vx_ref, vl_ref, o_ref, *, n_g):
    kx, kl, vx, vl = kx_ref[...], kl_ref[...], vx_ref[...], vl_ref[...]
    for g in range(n_g):
        sl = slice(g * HEAD_DIM, (g + 1) * HEAD_DIM)
        q = q_ref[:, sl]
        o_ref[:, sl] = _softmax_pv([_dot_nt(q, kx), _dot_nt(q, kl)], [vx, vl], None)


def _attn_glob(qc, kc, vc, dims):
    n_batch, seq, ctx_len, _ = dims
    t_rows = n_batch * seq
    tq = _pick_tile(256, seq)
    nq = seq // tq
    n_g = C_HEADS // C_KV
    lat_ctx_blocks = n_batch * seq // ctx_len

    def q_map(b, k, i):
        return (b * nq + i, k)

    kv_specs = [pl.BlockSpec((ctx_len, HEAD_DIM), lambda b, k, i: (lat_ctx_blocks + b, k)),
                pl.BlockSpec((seq, HEAD_DIM), lambda b, k, i: (b, k))]
    return pl.pallas_call(
        functools.partial(_attn_glob_kernel, n_g=n_g),
        grid=(n_batch, C_KV, nq),
        in_specs=[pl.BlockSpec((tq, n_g * HEAD_DIM), q_map)] + kv_specs + kv_specs,
        out_specs=pl.BlockSpec((tq, n_g * HEAD_DIM), q_map),
        out_shape=jax.ShapeDtypeStruct((t_rows, C_WIDTH), F32),
        compiler_params=_params(("arbitrary",) * 3),
        name="attn_global",
    )(qc, kc, kc, vc, vc)


def _attn_ctx_kernel(sink_ref, q_ref, k_ref, v_ref, o_ref, *, n_g, use_sink):
    kvh = pl.program_id(1)
    k, v = k_ref[...], v_ref[...]
    for g in range(n_g):
        sl = slice(g * HEAD_DIM, (g + 1) * HEAD_DIM)
        sink = sink_ref[kvh * n_g + g] if use_sink else None
        o_ref[:, sl] = _softmax_pv([_dot_nt(q_ref[:, sl], k)], [v], sink)


def _attn_ctx(sink, q, k, v, n_heads, n_kv, use_sink, dims):
    n_batch, seq, ctx_len, _ = dims
    n_g = n_heads // n_kv
    lat_ctx_blocks = n_batch * seq // ctx_len

    def blk(b, k):
        return (lat_ctx_blocks + b, k)

    return pl.pallas_call(
        functools.partial(_attn_ctx_kernel, n_g=n_g, use_sink=use_sink),
        grid=(n_batch, n_kv),
        in_specs=[pl.BlockSpec(memory_space=pltpu.SMEM),
                  pl.BlockSpec((ctx_len, n_g * HEAD_DIM), blk),
                  pl.BlockSpec((ctx_len, HEAD_DIM), blk),
                  pl.BlockSpec((ctx_len, HEAD_DIM), blk)],
        out_specs=pl.BlockSpec((ctx_len, n_g * HEAD_DIM), lambda b, k: (b, k)),
        out_shape=jax.ShapeDtypeStruct((n_batch * ctx_len, n_heads * HEAD_DIM), F32),
        compiler_params=_params(("arbitrary",) * 2),
        name="attn_context",
    )(sink, q, k, v)


def _merge_kernel(*refs, n_lat_tiles, has_ctx):
    if has_ctx:
        oa_ref, oc_ref, oax_ref, ocx_ref = refs[:4]
        refs = refs[4:]
    else:
        oa_ref, oc_ref = refs[:2]
        refs = refs[2:]
    ob_ref, og_ref, w_ref, x_ref, g1_ref, sh_ref, sc_ref, n2_ref, wr_ref, xo_ref, h_ref, aff_ref = refs
    is_lat = pl.program_id(0) < n_lat_tiles

    def group(o, off, width):
        y = (_rms_scale(o) * og_ref[:, off:off + width]).astype(BF16)
        return jnp.dot(y, w_ref[off:off + width, :], preferred_element_type=F32)

    o_a, o_c = oa_ref[...], oc_ref[...]
    if has_ctx:
        o_a = jnp.where(is_lat, o_a, oax_ref[...])
        o_c = jnp.where(is_lat, o_c, ocx_ref[...])
    y = group(o_a, 0, A_WIDTH) + group(ob_ref[...], A_WIDTH, B_WIDTH) + group(o_c, A_WIDTH + B_WIDTH, C_WIDTH)
    xn = x_ref[...] + g1_ref[0] * y
    xo_ref[...] = xn
    h = _rms_scale(xn) * n2_ref[...]
    h = h * (1.0 + sc_ref[0]) + sh_ref[0]
    hb = h.astype(BF16)
    half = h.shape[1] // 2
    lo = lax.bitcast_convert_type(hb[:, :half].astype(F32), U32) >> 16
    hi = lax.bitcast_convert_type(hb[:, half:].astype(F32), U32) & jnp.uint32(0xFFFF0000)
    h_ref[...] = lo | hi
    w = wr_ref[...]
    w_hi = w.astype(BF16)
    w_lo = (w - w_hi.astype(F32)).astype(BF16)
    h_lo = (h - hb.astype(F32)).astype(BF16)
    logits = (jnp.dot(hb, w_hi, preferred_element_type=F32) + jnp.dot(hb, w_lo, preferred_element_type=F32)
              + jnp.dot(h_lo, w_hi, preferred_element_type=F32))
    e = jnp.exp(logits - logits.max(axis=-1, keepdims=True))
    aff_ref[...] = e / jnp.sum(e, axis=-1, keepdims=True)


def _merge(o_a, o_c, o_ax, o_cx, o_b, out_g, w_out_bf, xs, modt, norm2_g, w_router, rows_out, dims):
    n_batch, seq, ctx_len, d = dims
    tm = _pick_tile(512, seq, n_batch * ctx_len)
    n_lat_tiles = n_batch * seq // tm
    tps = seq // tm
    n_exp = w_router.shape[1]
    mix = A_WIDTH + B_WIDTH + C_WIDTH
    has_ctx = o_ax is not None
    assert has_ctx == (rows_out > n_batch * seq)

    def mod_spec(k):
        return pl.BlockSpec((1, 1, d), lambda i: (_row_seg(i, n_lat_tiles, tps, n_batch) * N_MOD + k, 0, 0))

    def rows(width):
        return pl.BlockSpec((tm, width), lambda i: (i, 0))

    def lat_rows(width):
        return pl.BlockSpec((tm, width), lambda i: (jnp.minimum(i, n_lat_tiles - 1), 0))

    def ctx_rows(width):
        return pl.BlockSpec((tm, width), lambda i: (jnp.maximum(i - n_lat_tiles, 0), 0))

    def full(shape):
        return pl.BlockSpec(shape, lambda i: (0,) * len(shape))

    attn_specs = [lat_rows(A_WIDTH), lat_rows(C_WIDTH)]
    attn_args = [o_a, o_c]
    if has_ctx:
        attn_specs += [ctx_rows(A_WIDTH), ctx_rows(C_WIDTH)]
        attn_args += [o_ax, o_cx]
    return pl.pallas_call(
        functools.partial(_merge_kernel, n_lat_tiles=n_lat_tiles, has_ctx=has_ctx),
        grid=(rows_out // tm,),
        in_specs=attn_specs + [rows(B_WIDTH), full((1, mix)), full((mix, d)), rows(d),
                               mod_spec(2), mod_spec(3), mod_spec(4), full((1, d)), full((d, n_exp))],
        out_specs=[rows(d), rows(d // 2), rows(n_exp)],
        out_shape=[jax.ShapeDtypeStruct((rows_out, d), F32), jax.ShapeDtypeStruct((rows_out, d // 2), U32),
                   jax.ShapeDtypeStruct((rows_out, n_exp), F32)],
        compiler_params=_params(("arbitrary",)),
        name="merge_router",
    )(*attn_args, o_b, out_g, w_out_bf, xs, modt, modt, modt, norm2_g, w_router)


def _prefix_excl(mask_f32, tri):
    rows, n = mask_f32.shape
    cw = tri.shape[0]
    carry = jnp.zeros((rows, 1), F32)
    parts = []
    for c in range(n // cw):
        m = mask_f32[:, c * cw:(c + 1) * cw]
        parts.append(jnp.dot(m.astype(BF16), tri, preferred_element_type=F32) + carry)
        carry = carry + jnp.sum(m, axis=-1, keepdims=True)
    return parts[0] if len(parts) == 1 else jnp.concatenate(parts, axis=-1)


def _route_kernel(aff_ref, idx_ref, gate_ref, *, n, cap, n_exp):
    a = aff_ref[0]
    bits = pltpu.bitcast(a, I32)

    def bit_step(b, thr):
        cand = thr | jnp.left_shift(jnp.int32(1), 30 - b)
        cnt = jnp.sum(jnp.where(bits >= cand, 1.0, 0.0), axis=-1, keepdims=True)
        return jnp.where(cnt >= cap, cand, thr)

    thr = lax.fori_loop(0, 31, bit_step, jnp.zeros((n_exp, 1), I32))
    gt = bits > thr
    eq = bits == thr
    cw = min(256, n)
    tri = jnp.where(lax.broadcasted_iota(I32, (cw, cw), 0) < lax.broadcasted_iota(I32, (cw, cw), 1),
                    1.0, 0.0).astype(BF16)
    need = cap - jnp.sum(jnp.where(gt, 1.0, 0.0), axis=-1, keepdims=True)
    sel = gt | (eq & (_prefix_excl(jnp.where(eq, 1.0, 0.0), tri) < need))
    pos = jnp.where(sel, _prefix_excl(jnp.where(sel, 1.0, 0.0), tri), -1.0)

    tok = lax.broadcasted_iota(I32, (1, n), 1)
    t_hi = (tok // 64).astype(F32)
    t_lo = (tok % 64).astype(F32)
    g_hi = a.astype(BF16).astype(F32)
    r1 = a - g_hi
    g_mid = r1.astype(BF16).astype(F32)
    g_lo = r1 - g_mid
    slot = lax.broadcasted_iota(I32, (cap, n), 0).astype(F32)
    row = lax.broadcasted_iota(I32, (16, n), 0)
    for e in range(n_exp):
        onehot = jnp.where(pos[e:e + 1, :] == slot, 1.0, 0.0).astype(BF16)
        lhs = jnp.where(row == 0, t_hi,
              jnp.where(row == 1, t_lo,
              jnp.where(row == 2, g_hi[e:e + 1, :],
              jnp.where(row == 3, g_mid[e:e + 1, :],
              jnp.where(row == 4, g_lo[e:e + 1, :], 0.0))))).astype(BF16)
        res = _dot_nt(lhs, onehot)
        idx_ref[0, e:e + 1, :] = (res[0:1, :] * 64.0 + res[1:2, :]).astype(I32)
        gate_ref[0, e:e + 1, :] = res[2:3, :] + res[3:4, :] + res[4:5, :]


def _route(aff_sets):
    n_sets, n_exp, n = aff_sets.shape
    cap = EC_CAPACITY * n // n_exp
    spec = pl.BlockSpec((1, n_exp, cap), lambda s: (s, 0, 0))
    return pl.pallas_call(
        functools.partial(_route_kernel, n=n, cap=cap, n_exp=n_exp),
        grid=(n_sets,),
        in_specs=[pl.BlockSpec((1, n_exp, n), lambda s: (s, 0, 0))],
        out_specs=[spec, spec],
        out_shape=[jax.ShapeDtypeStruct((n_sets, n_exp, cap), I32),
                   jax.ShapeDtypeStruct((n_sets, n_exp, cap), F32)],
        compiler_params=_params(("arbitrary",)),
        name="route",
    )(aff_sets)


def _ffn_kernel(idx_ref, hpk_hbm, gate_ref, g2_ref, wg_ref, wu_ref, wd_ref, x_in_hbm, x_hbm,
                hpk, hb, yacc, xbuf, sem_h, sem_x, sem_s, *, m, n_f, n_groups, groups, n_split, per, ctx_seg):
    del x_in_hbm
    e = pl.program_id(0)
    part = pl.program_id(1)
    f = pl.program_id(2)
    g = e * n_split + part
    slot = g % 2
    g_prev = jnp.maximum(g - 1, 0)
    g_next = (g + 1) % n_groups
    half = n_f // 2
    rows_rw = m // half
    rows_h = m // n_f
    d_half = hpk.shape[2]

    def h_copy(grp, j, dst_slot):
        r = idx_ref[grp * m + j]
        return pltpu.make_async_copy(hpk_hbm.at[pl.ds(r, 1), :], hpk.at[dst_slot, pl.ds(j, 1), :],
                                     sem_h.at[dst_slot])

    def x_copy(grp, j):
        r = idx_ref[grp * m + j]
        return pltpu.make_async_copy(x_hbm.at[pl.ds(r, 1), :], xbuf.at[pl.ds(j, 1), :], sem_x)

    def s_copy(grp, j):
        r = idx_ref[grp * m + j]
        return pltpu.make_async_copy(xbuf.at[pl.ds(j, 1), :], x_hbm.at[pl.ds(r, 1), :], sem_s)

    def wait_h(s):
        pltpu.make_async_copy(hpk.at[1 - s], hpk.at[s], sem_h.at[s]).wait()

    def wait_rows(sem):
        pltpu.make_async_copy(yacc, xbuf, sem).wait()

    def for_rows(fn):
        def body(j, carry):
            fn(j)
            return carry
        lax.fori_loop(0, m, body, 0)

    @pl.when((g == 0) & (f == 0))
    def _():
        for_rows(lambda j: h_copy(0, j, 0).start())
        for_rows(lambda j: x_copy(0, j).start())
        wait_rows(sem_x)

    @pl.when(f == 0)
    def _():
        wait_h(slot)
        p = hpk[slot]
        hb[:, :d_half] = lax.bitcast_convert_type(p << 16, F32).astype(BF16)
        hb[:, d_half:] = lax.bitcast_convert_type(p & jnp.uint32(0xFFFF0000), F32).astype(BF16)
        yacc[...] = jnp.zeros_like(yacc)

    @pl.when(f == half)
    def _():
        wait_rows(sem_s)

    def matmuls():
        h = hb[...]
        a = jnp.dot(h, wg_ref[...].astype(BF16), preferred_element_type=F32)
        u = jnp.dot(h, wu_ref[...].astype(BF16), preferred_element_type=F32)
        z = (a / (1.0 + jnp.exp(-a)) * u).astype(BF16)
        yacc[...] += jnp.dot(z, wd_ref[...].astype(BF16), preferred_element_type=F32)

    def fetch_next_inputs():
        for j in range(rows_h):
            h_copy(g_next, f * rows_h + j, 1 - slot).start()

    @pl.when(f < half)
    def _():
        matmuls()
        for j in range(rows_rw):
            s_copy(g_prev, f * rows_rw + j).start()
        fetch_next_inputs()

    @pl.when(f >= half)
    def _():
        matmuls()
        for j in range(rows_rw):
            x_copy(g, (f - half) * rows_rw + j).start()
        fetch_next_inputs()

    @pl.when(f == n_f - 1)
    def _():
        wait_rows(sem_x)
        for off, size, sample in groups:
            seg = ctx_seg if sample is None else part * per + sample
            rows = pl.ds(off, size)
            xbuf[rows, :] = xbuf[rows, :] + g2_ref[pl.ds(seg, 1), :] * (gate_ref[0, 0, rows, :] * yacc[rows, :])

    @pl.when((g == n_groups - 1) & (f == n_f - 1))
    def _():
        for_rows(lambda j: s_copy(g, j).start())
        wait_rows(sem_s)
        wait_h(1 - slot)


def _experts(idx_flat, gate, hpk, g2tab, w_gate, w_up, w_down, xs, layer, m, groups, n_split, per, ctx_seg):
    d = xs.shape[1]
    n_exp, d_exp = w_gate.shape[1], w_gate.shape[3]
    tf = _pick_tile(256, d_exp // 2)
    n_f = d_exp // tf
    n_groups = n_exp * n_split
    assert n_f % 2 == 0 and m % n_f == 0 and n_groups % 2 == 0
    grid_spec = pltpu.PrefetchScalarGridSpec(
        num_scalar_prefetch=1,
        grid=(n_exp, n_split, n_f),
        in_specs=[
            pl.BlockSpec(memory_space=pl.ANY),
            pl.BlockSpec((1, 1, m, 1), lambda e, p, f, idx: (e, p, 0, 0)),
            pl.BlockSpec(g2tab.shape, lambda e, p, f, idx: (0, 0)),
            pl.BlockSpec((None, None, d, tf), lambda e, p, f, idx: (layer, e, 0, f)),
            pl.BlockSpec((None, None, d, tf), lambda e, p, f, idx: (layer, e, 0, f)),
            pl.BlockSpec((None, None, tf, d), lambda e, p, f, idx: (layer, e, f, 0)),
            pl.BlockSpec(memory_space=pl.ANY),
        ],
        out_specs=pl.BlockSpec(memory_space=pl.ANY),
        scratch_shapes=[pltpu.VMEM((2, m, d // 2), U32), pltpu.VMEM((m, d), BF16), pltpu.VMEM((m, d), F32),
                        pltpu.VMEM((m, d), F32), pltpu.SemaphoreType.DMA((2,)), pltpu.SemaphoreType.DMA,
                        pltpu.SemaphoreType.DMA],
    )
    return pl.pallas_call(
        functools.partial(_ffn_kernel, m=m, n_f=n_f, n_groups=n_groups, groups=tuple(groups), n_split=n_split,
                          per=per, ctx_seg=ctx_seg),
        grid_spec=grid_spec,
        out_shape=jax.ShapeDtypeStruct(xs.shape, xs.dtype),
        input_output_aliases={7: 0},
        compiler_params=_params(("arbitrary",) * 3),
        name="experts",
    )(idx_flat, hpk, gate, g2tab, w_gate, w_up, w_down, xs)


def _rope_tables(seq):
    pos = np.arange(seq)
    n_freq = HEAD_DIM // 4
    inv = jnp.asarray(ROPE_THETA, F32) ** (-jnp.arange(n_freq, dtype=F32) / n_freq)
    row = jnp.asarray(pos // GRID_W, F32)[:, None] * inv
    col = jnp.asarray(pos % GRID_W, F32)[:, None] * inv
    zero = jnp.zeros_like(row)
    cos = jnp.concatenate([jnp.cos(row), jnp.cos(row), jnp.cos(col), jnp.cos(col)], axis=-1)
    s1 = jnp.concatenate([zero, jnp.sin(row), zero, jnp.sin(col)], axis=-1)
    s2 = jnp.concatenate([-jnp.sin(row), zero, -jnp.sin(col), zero], axis=-1)

    def padded(tm):
        ident = jnp.ones((tm, HEAD_DIM), F32)
        none = jnp.zeros((tm, HEAD_DIM), F32)
        return (jnp.concatenate([cos, ident], axis=0), jnp.concatenate([s1, none], axis=0),
                jnp.concatenate([s2, none], axis=0))

    return padded


def _expert_rows(idx_l, gate_l, idx_c, gate_c, dims, n_split):
    n_batch, seq, ctx_len, _ = dims
    per = n_batch // n_split
    n_exp = idx_l.shape[1]

    def arrange(v):
        s = v.shape[-1]
        return v.reshape(n_split, per, n_exp, s).transpose(2, 0, 1, 3).reshape(n_exp, n_split, per * s)

    rows_l = idx_l + (jnp.arange(n_batch, dtype=I32) * seq)[:, None, None]
    parts_i, parts_g = [arrange(rows_l)], [arrange(gate_l)]
    groups = [(b * idx_l.shape[-1], idx_l.shape[-1], b) for b in range(per)]
    if idx_c is not None:
        rows_c = idx_c + (n_batch * seq + jnp.arange(n_batch, dtype=I32) * ctx_len)[:, None, None]
        parts_i.append(arrange(rows_c))
        parts_g.append(arrange(gate_c))
        groups += [(per * idx_l.shape[-1], per * idx_c.shape[-1], None)]
    idx = jnp.concatenate(parts_i, axis=-1)
    gate = jnp.concatenate(parts_g, axis=-1)
    return idx, gate, groups


def kernel(x, c, ctx, c_ctx, w_mod, b_mod, norm1_g, norm2_g, w_in, qn_a, kn_a, sink_a, vn_b, w_s, b_s,
           qn_c, kn_c, out_g, w_out, w_router, w_gate, w_up, w_down):
    n_batch, seq, d = x.shape
    ctx_len = ctx.shape[1]
    depth = w_mod.shape[0]
    n_exp = w_router.shape[2]
    dims = (n_batch, seq, ctx_len, d)
    t_lat = n_batch * seq
    assert n_batch < MOD_ROWS and seq % WINDOW == 0 and ctx_len % B_CHUNK == 0 and t_lat % ctx_len == 0
    n_split = 2 if n_batch % 2 == 0 else 1
    per = n_batch // n_split

    xs = jnp.concatenate([x.reshape(t_lat, d), ctx.reshape(n_batch * ctx_len, d)], axis=0)
    cvec = jnp.zeros((MOD_ROWS, d), F32).at[:n_batch].set(c).at[n_batch].set(c_ctx)
    rope = _rope_tables(seq)

    for layer in range(depth):
        last = layer == depth - 1
        mod = _modulation(cvec, w_mod, b_mod, layer)
        modt = mod.reshape(MOD_ROWS * N_MOD, 1, d)
        g2tab = mod.reshape(MOD_ROWS, N_MOD, d)[:, N_MOD - 1, :]
        row = lambda v: v[layer].reshape(1, -1)
        bsb = jnp.broadcast_to(b_s[layer][:, :, None], (B_GROUPS, B_CHUNK, HEAD_DIM))
        qa, ka, va, qc, kc, vc, o_b = _inproj(
            xs, modt, row(norm1_g), w_in[layer].astype(BF16), rope, row(qn_a), row(kn_a), row(qn_c),
            row(kn_c), row(vn_b), w_s[layer].astype(BF16), bsb, dims)
        o_a = _attn_win(sink_a[layer], qa, ka, va, dims)
        o_c = _attn_glob(qc, kc, vc, dims)
        o_ax = o_cx = None
        if not last:
            o_ax = _attn_ctx(sink_a[layer], qa, ka, va, A_HEADS, A_KV, True, dims)
            o_cx = _attn_ctx(sink_a[layer], qc, kc, vc, C_HEADS, C_KV, False, dims)
        rows_out = t_lat if last else xs.shape[0]
        xs, h2, aff = _merge(o_a, o_c, o_ax, o_cx, o_b, row(out_g), w_out[layer].astype(BF16), xs, modt,
                             row(norm2_g), w_router[layer], rows_out, dims)

        idx_l, gate_l = _route(aff[:t_lat].reshape(n_batch, seq, n_exp).transpose(0, 2, 1))
        idx_c = gate_c = None
        if not last:
            idx_c, gate_c = _route(aff[t_lat:].reshape(n_batch, ctx_len, n_exp).transpose(0, 2, 1))
        idx, gate, groups = _expert_rows(idx_l, gate_l, idx_c, gate_c, dims, n_split)
        m = idx.shape[-1]
        xs = _experts(idx.reshape(-1), gate.reshape(n_exp, n_split, m, 1), h2, g2tab, w_gate, w_up, w_down,
                      xs, layer, m, groups, n_split, per, n_batch)
    return xs.reshape(n_batch, seq, d)
```

```python
import functools

import jax
import jax.numpy as jnp
import numpy as np
from jax import lax
from jax.experimental import pallas as pl
from jax.experimental.pallas import tpu as pltpu

F32 = jnp.float32
BF16 = jnp.bfloat16
I32 = jnp.int32
U32 = jnp.uint32

HEAD_DIM = 128
GRID_W = 64
WINDOW = 128
A_HEADS, A_KV = 6, 2
B_GROUPS, B_CHUNK = 4, 128
C_HEADS, C_KV = 6, 2
A_WIDTH = A_HEADS * HEAD_DIM
B_WIDTH = B_GROUPS * HEAD_DIM
C_WIDTH = C_HEADS * HEAD_DIM
KV_WIDTH = A_KV * HEAD_DIM
OFF_QA = 0
OFF_KA = OFF_QA + A_WIDTH
OFF_VA = OFF_KA + KV_WIDTH
OFF_UB = OFF_VA + KV_WIDTH
OFF_VB = OFF_UB + B_WIDTH
OFF_QC = OFF_VB + B_WIDTH
OFF_KC = OFF_QC + C_WIDTH
OFF_VC = OFF_KC + KV_WIDTH
IN_WIDTH = OFF_VC + KV_WIDTH
EC_CAPACITY = 2
N_MOD = 6
ROPE_THETA = 10000.0
EPS = 1e-6
NEG_INF = -1e30
QK_SCALE = HEAD_DIM ** -0.5
MOD_ROWS = 8

VMEM_LIMIT = 56 * 1024 * 1024

def _params(sem, vmem=VMEM_LIMIT):
    return pltpu.CompilerParams(dimension_semantics=sem, vmem_limit_bytes=vmem)


def _pick_tile(target, *sizes):
    t = target
    while any(s % t for s in sizes):
        t //= 2
    return t


def _rms_scale(x):
    return x * lax.rsqrt(jnp.mean(x * x, axis=-1, keepdims=True) + EPS)


def _dot_nt(a, b):
    return lax.dot_general(a, b, (((1,), (1,)), ((), ())), preferred_element_type=F32)


def _mod_kernel(c_ref, w_ref, b_ref, o_ref):
    c = c_ref[...]
    s = c / (1.0 + jnp.exp(-c))
    o_ref[...] = jnp.dot(s, w_ref[...], precision=lax.Precision.HIGHEST,
                         preferred_element_type=F32) + b_ref[...]


def _modulation(cvec, w_mod, b_mod, layer):
    d = cvec.shape[1]
    n = w_mod.shape[2]
    tn = _pick_tile(1024, n)
    return pl.pallas_call(
        _mod_kernel,
        grid=(n // tn,),
        in_specs=[
            pl.BlockSpec((MOD_ROWS, d), lambda j: (0, 0)),
            pl.BlockSpec((None, d, tn), lambda j: (layer, 0, j)),
            pl.BlockSpec((None, 1, tn), lambda j: (layer, 0, j)),
        ],
        out_specs=pl.BlockSpec((MOD_ROWS, tn), lambda j: (0, j)),
        out_shape=jax.ShapeDtypeStruct((MOD_ROWS, n), F32),
        compiler_params=_params(("arbitrary",)),
        name="modulation",
    )(cvec, w_mod, b_mod.reshape(b_mod.shape[0], 1, n))


def _inproj_kernel(x_ref, sh_ref, sc_ref, g_ref, w_ref, cos_ref, s1_ref, s2_ref,
                   qna_ref, kna_ref, qnc_ref, knc_ref, vnb_ref, ws_ref, bsb_ref,
                   qa_ref, ka_ref, va_ref, qc_ref, kc_ref, vc_ref, ob_ref, *, tm):
    x = x_ref[...]
    h = _rms_scale(x) * g_ref[...]
    h = h * (1.0 + sc_ref[0]) + sh_ref[0]
    hb = h.astype(BF16)
    cos, s1, s2 = cos_ref[...], s1_ref[...], s2_ref[...]

    def proj(off, width):
        return jnp.dot(hb, w_ref[:, off:off + width], preferred_element_type=F32)

    def qk_head(p, gain, scale):
        q = _rms_scale(p) * gain
        q = q * cos + pltpu.roll(q, 32, 1) * s1 + pltpu.roll(q, 96, 1) * s2
        if scale != 1.0:
            q = q * scale
        return q.astype(BF16)

    def heads(off, n_heads, gain_ref, scale, out_ref):
        p = proj(off, n_heads * HEAD_DIM)
        for hd in range(n_heads):
            sl = slice(hd * HEAD_DIM, (hd + 1) * HEAD_DIM)
            out_ref[:, sl] = qk_head(p[:, sl], gain_ref[...], scale)

    heads(OFF_QA, A_HEADS, qna_ref, QK_SCALE, qa_ref)
    heads(OFF_KA, A_KV, kna_ref, 1.0, ka_ref)
    va_ref[...] = proj(OFF_VA, KV_WIDTH).astype(BF16)
    heads(OFF_QC, C_HEADS, qnc_ref, QK_SCALE, qc_ref)
    heads(OFF_KC, C_KV, knc_ref, 1.0, kc_ref)
    vc_ref[...] = proj(OFF_VC, KV_WIDTH).astype(BF16)

    pu = proj(OFF_UB, B_WIDTH)
    pv = proj(OFF_VB, B_WIDTH)
    for g in range(B_GROUPS):
        sl = slice(g * HEAD_DIM, (g + 1) * HEAD_DIM)
        u = jax.nn.gelu(pu[:, sl])
        vn = (_rms_scale(jax.nn.gelu(pv[:, sl])) * vnb_ref[:, sl]).astype(BF16)
        for c in range(tm // B_CHUNK):
            rows = slice(c * B_CHUNK, (c + 1) * B_CHUNK)
            mixed = jnp.dot(ws_ref[g], vn[rows, :], preferred_element_type=F32) + bsb_ref[g]
            ob_ref[rows, sl] = u[rows, :] * mixed


def _row_seg(i, n_lat_tiles, tiles_per_seq, n_batch):
    return jnp.where(i < n_lat_tiles, i // tiles_per_seq, n_batch)


def _inproj(xs, modt, norm_g, w_in_bf, rope, qn_a, kn_a, qn_c, kn_c, vn_b, ws_bf, bsb, dims):
    n_batch, seq, ctx_len, d = dims
    t_rows = xs.shape[0]
    tm = _pick_tile(512, seq, n_batch * ctx_len)
    n_lat_tiles = n_batch * seq // tm
    tps = seq // tm
    cos_t, s1_t, s2_t = rope(tm)

    def mod_spec(k):
        return pl.BlockSpec((1, 1, d), lambda i: (_row_seg(i, n_lat_tiles, tps, n_batch) * N_MOD + k, 0, 0))

    def rope_spec():
        return pl.BlockSpec((tm, HEAD_DIM), lambda i: (jnp.where(i < n_lat_tiles, i % tps, tps), 0))

    def full(shape):
        return pl.BlockSpec(shape, lambda i: (0,) * len(shape))

    def out(width, dtype):
        return pl.BlockSpec((tm, width), lambda i: (i, 0)), jax.ShapeDtypeStruct((t_rows, width), dtype)

    outs = [out(A_WIDTH, BF16), out(KV_WIDTH, BF16), out(KV_WIDTH, BF16),
            out(C_WIDTH, BF16), out(KV_WIDTH, BF16), out(KV_WIDTH, BF16), out(B_WIDTH, F32)]
    return pl.pallas_call(
        functools.partial(_inproj_kernel, tm=tm),
        grid=(t_rows // tm,),
        in_specs=[
            pl.BlockSpec((tm, d), lambda i: (i, 0)),
            mod_spec(0), mod_spec(1),
            full((1, d)),
            full((d, IN_WIDTH)),
            rope_spec(), rope_spec(), rope_spec(),
            full((1, HEAD_DIM)), full((1, HEAD_DIM)), full((1, HEAD_DIM)), full((1, HEAD_DIM)),
            full((1, B_WIDTH)),
            full((B_GROUPS, B_CHUNK, B_CHUNK)),
            full((B_GROUPS, B_CHUNK, HEAD_DIM)),
        ],
        out_specs=[o[0] for o in outs],
        out_shape=[o[1] for o in outs],
        compiler_params=_params(("arbitrary",)),
        name="inproj",
    )(xs, modt, modt, norm_g, w_in_bf, cos_t, s1_t, s2_t, qn_a, kn_a, qn_c, kn_c, vn_b, ws_bf, bsb)


def _softmax_pv(s_list, v_list, sink):
    m = s_list[0].max(axis=-1, keepdims=True)
    for s in s_list[1:]:
        m = jnp.maximum(m, s.max(axis=-1, keepdims=True))
    if sink is not None:
        m = jnp.maximum(m, sink)
    l = None
    o = None
    for s, v in zip(s_list, v_list):
        p = jnp.exp(s - m)
        ls = jnp.sum(p, axis=-1, keepdims=True)
        os_ = jnp.dot(p.astype(BF16), v, preferred_element_type=F32)
        l = ls if l is None else l + ls
        o = os_ if o is None else o + os_
    if sink is not None:
        l = l + jnp.exp(sink - m)
    return o / l


def _attn_win_kernel(sink_ref, q_ref, kx_ref, kp_ref, km_ref, kn_ref, vx_ref, vp_ref, vm_ref, vn_ref,
                     o_ref, *, tq, seq, n_g):
    i = pl.program_id(1)
    kvh = pl.program_id(2)
    ctx_len = kx_ref.shape[0]
    kband = jnp.concatenate([kp_ref[...], km_ref[...], kn_ref[...]], axis=0)
    vband = jnp.concatenate([vp_ref[...], vm_ref[...], vn_ref[...]], axis=0)
    nb = tq + 2 * WINDOW
    r = lax.broadcasted_iota(I32, (tq, nb), 0)
    c = lax.broadcasted_iota(I32, (tq, nb), 1)
    kpos = c - WINDOW + i * tq
    qpos = r + i * tq
    valid = (jnp.abs(kpos - qpos) <= WINDOW) & (kpos >= 0) & (kpos < seq)
    kx, vx = kx_ref[...], vx_ref[...]
    for g in range(n_g):
        sl = slice(g * HEAD_DIM, (g + 1) * HEAD_DIM)
        q = q_ref[:, sl]
        s_ctx = _dot_nt(q, kx)
        s_band = jnp.where(valid, _dot_nt(q, kband), NEG_INF)
        o_ref[:, sl] = _softmax_pv([s_ctx, s_band], [vx, vband], sink_ref[kvh * n_g + g])
    del ctx_len


def _attn_win(sink, qa, ka, va, dims):
    n_batch, seq, ctx_len, _ = dims
    t_rows = n_batch * seq
    tq = _pick_tile(512, seq)
    nq = seq // tq
    n_g = A_HEADS // A_KV
    wb = tq // WINDOW
    sb = seq // WINDOW
    lat_ctx_blocks = n_batch * seq // ctx_len

    def q_map(b, i, k):
        return (b * nq + i, k)

    def prev_map(b, i, k):
        return (b * sb + jnp.maximum(i * wb - 1, 0), k)

    def next_map(b, i, k):
        return (b * sb + jnp.minimum((i + 1) * wb, sb - 1), k)

    def ctx_map(b, i, k):
        return (lat_ctx_blocks + b, k)

    kv_specs = [pl.BlockSpec((ctx_len, HEAD_DIM), ctx_map), pl.BlockSpec((WINDOW, HEAD_DIM), prev_map),
                pl.BlockSpec((tq, HEAD_DIM), q_map), pl.BlockSpec((WINDOW, HEAD_DIM), next_map)]
    return pl.pallas_call(
        functools.partial(_attn_win_kernel, tq=tq, seq=seq, n_g=n_g),
        grid=(n_batch, nq, A_KV),
        in_specs=[pl.BlockSpec(memory_space=pltpu.SMEM),
                  pl.BlockSpec((tq, n_g * HEAD_DIM), q_map)] + kv_specs + kv_specs,
        out_specs=pl.BlockSpec((tq, n_g * HEAD_DIM), q_map),
        out_shape=jax.ShapeDtypeStruct((t_rows, A_WIDTH), F32),
        compiler_params=_params(("arbitrary",) * 3),
        name="attn_window",
    )(sink, qa, ka, ka, ka, ka, va, va, va, va)


def _attn_glob_kernel(q_ref, kx_ref, kl_ref, vx_ref, vl_ref, o_ref, *, n_g):
    kx, kl, vx, vl = kx_ref[...], kl_ref[...], vx_ref[...], vl_ref[...]
    for g in range(n_g):
        sl = slice(g * HEAD_DIM, (g + 1) * HEAD_DIM)
        q = q_ref[:, sl]
        o_ref[:, sl] = _softmax_pv([_dot_nt(q, kx), _dot_nt(q, kl)], [vx, vl], None)


def _attn_glob(qc, kc, vc, dims):
    n_batch, seq, ctx_len, _ = dims
    t_rows = n_batch * seq
    tq = _pick_tile(256, seq)
    nq = seq // tq
    n_g = C_HEADS // C_KV
    lat_ctx_blocks = n_batch * seq // ctx_len

    def q_map(b, k, i):
        return (b * nq + i, k)

    kv_specs = [pl.BlockSpec((ctx_len, HEAD_DIM), lambda b, k, i: (lat_ctx_blocks + b, k)),
                pl.BlockSpec((seq, HEAD_DIM), lambda b, k, i: (b, k))]
    return pl.pallas_call(
        functools.partial(_attn_glob_kernel, n_g=n_g),
        grid=(n_batch, C_KV, nq),
        in_specs=[pl.BlockSpec((tq, n_g * HEAD_DIM), q_map)] + kv_specs + kv_specs,
        out_specs=pl.BlockSpec((tq, n_g * HEAD_DIM), q_map),
        out_shape=jax.ShapeDtypeStruct((t_rows, C_WIDTH), F32),
        compiler_params=_params(("arbitrary",) * 3),
        name="attn_global",
    )(qc, kc, kc, vc, vc)


def _attn_ctx_kernel(sink_ref, q_ref, k_ref, v_ref, o_ref, *, n_g, use_sink):
    kvh = pl.program_id(1)
    k, v = k_ref[...], v_ref[...]
    for g in range(n_g):
        sl = slice(g * HEAD_DIM, (g + 1) * HEAD_DIM)
        sink = sink_ref[kvh * n_g + g] if use_sink else None
        o_ref[:, sl] = _softmax_pv([_dot_nt(q_ref[:, sl], k)], [v], sink)


def _attn_ctx(sink, q, k, v, n_heads, n_kv, use_sink, dims):
    n_batch, seq, ctx_len, _ = dims
    n_g = n_heads // n_kv
    lat_ctx_blocks = n_batch * seq // ctx_len

    def blk(b, k):
        return (lat_ctx_blocks + b, k)

    return pl.pallas_call(
        functools.partial(_attn_ctx_kernel, n_g=n_g, use_sink=use_sink),
        grid=(n_batch, n_kv),
        in_specs=[pl.BlockSpec(memory_space=pltpu.SMEM),
                  pl.BlockSpec((ctx_len, n_g * HEAD_DIM), blk),
                  pl.BlockSpec((ctx_len, HEAD_DIM), blk),
                  pl.BlockSpec((ctx_len, HEAD_DIM), blk)],
        out_specs=pl.BlockSpec((ctx_len, n_g * HEAD_DIM), lambda b, k: (b, k)),
        out_shape=jax.ShapeDtypeStruct((n_batch * ctx_len, n_heads * HEAD_DIM), F32),
        compiler_params=_params(("arbitrary",) * 2),
        name="attn_context",
    )(sink, q, k, v)


def _merge_kernel(*refs, n_lat_tiles, has_ctx):
    if has_ctx:
        oa_ref, oc_ref, oax_ref, ocx_ref = refs[:4]
        refs = refs[4:]
    else:
        oa_ref, oc_ref = refs[:2]
        refs = refs[2:]
    ob_ref, og_ref, w_ref, x_ref, g1_ref, sh_ref, sc_ref, n2_ref, wr_ref, xo_ref, h_ref, aff_ref = refs
    is_lat = pl.program_id(0) < n_lat_tiles

    def group(o, off, width):
        y = (_rms_scale(o) * og_ref[:, off:off + width]).astype(BF16)
        return jnp.dot(y, w_ref[off:off + width, :], preferred_element_type=F32)

    o_a, o_c = oa_ref[...], oc_ref[...]
    if has_ctx:
        o_a = jnp.where(is_lat, o_a, oax_ref[...])
        o_c = jnp.where(is_lat, o_c, ocx_ref[...])
    y = group(o_a, 0, A_WIDTH) + group(ob_ref[...], A_WIDTH, B_WIDTH) + group(o_c, A_WIDTH + B_WIDTH, C_WIDTH)
    xn = x_ref[...] + g1_ref[0] * y
    xo_ref[...] = xn
    h = _rms_scale(xn) * n2_ref[...]
    h = h * (1.0 + sc_ref[0]) + sh_ref[0]
    hb = h.astype(BF16)
    half = h.shape[1] // 2
    lo = lax.bitcast_convert_type(hb[:, :half].astype(F32), U32) >> 16
    hi = lax.bitcast_convert_type(hb[:, half:].astype(F32), U32) & jnp.uint32(0xFFFF0000)
    h_ref[...] = lo | hi
    w = wr_ref[...]
    w_hi = w.astype(BF16)
    w_lo = (w - w_hi.astype(F32)).astype(BF16)
    h_lo = (h - hb.astype(F32)).astype(BF16)
    logits = (jnp.dot(hb, w_hi, preferred_element_type=F32) + jnp.dot(hb, w_lo, preferred_element_type=F32)
              + jnp.dot(h_lo, w_hi, preferred_element_type=F32))
    e = jnp.exp(logits - logits.max(axis=-1, keepdims=True))
    aff_ref[...] = e / jnp.sum(e, axis=-1, keepdims=True)


def _merge(o_a, o_c, o_ax, o_cx, o_b, out_g, w_out_bf, xs, modt, norm2_g, w_router, rows_out, dims):
    n_batch, seq, ctx_len, d = dims
    tm = _pick_tile(512, seq, n_batch * ctx_len)
    n_lat_tiles = n_batch * seq // tm
    tps = seq // tm
    n_exp = w_router.shape[1]
    mix = A_WIDTH + B_WIDTH + C_WIDTH
    has_ctx = o_ax is not None
    assert has_ctx == (rows_out > n_batch * seq)

    def mod_spec(k):
        return pl.BlockSpec((1, 1, d), lambda i: (_row_seg(i, n_lat_tiles, tps, n_batch) * N_MOD + k, 0, 0))

    def rows(width):
        return pl.BlockSpec((tm, width), lambda i: (i, 0))

    def lat_rows(width):
        return pl.BlockSpec((tm, width), lambda i: (jnp.minimum(i, n_lat_tiles - 1), 0))

    def ctx_rows(width):
        return pl.BlockSpec((tm, width), lambda i: (jnp.maximum(i - n_lat_tiles, 0), 0))

    def full(shape):
        return pl.BlockSpec(shape, lambda i: (0,) * len(shape))

    attn_specs = [lat_rows(A_WIDTH), lat_rows(C_WIDTH)]
    attn_args = [o_a, o_c]
    if has_ctx:
        attn_specs += [ctx_rows(A_WIDTH), ctx_rows(C_WIDTH)]
        attn_args += [o_ax, o_cx]
    return pl.pallas_call(
        functools.partial(_merge_kernel, n_lat_tiles=n_lat_tiles, has_ctx=has_ctx),
        grid=(rows_out // tm,),
        in_specs=attn_specs + [rows(B_WIDTH), full((1, mix)), full((mix, d)), rows(d),
                               mod_spec(2), mod_spec(3), mod_spec(4), full((1, d)), full((d, n_exp))],
        out_specs=[rows(d), rows(d // 2), rows(n_exp)],
        out_shape=[jax.ShapeDtypeStruct((rows_out, d), F32), jax.ShapeDtypeStruct((rows_out, d // 2), U32),
                   jax.ShapeDtypeStruct((rows_out, n_exp), F32)],
        compiler_params=_params(("arbitrary",)),
        name="merge_router",
    )(*attn_args, o_b, out_g, w_out_bf, xs, modt, modt, modt, norm2_g, w_router)


def _prefix_excl(mask_f32, tri):
    rows, n = mask_f32.shape
    cw = tri.shape[0]
    carry = jnp.zeros((rows, 1), F32)
    parts = []
    for c in range(n // cw):
        m = mask_f32[:, c * cw:(c + 1) * cw]
        parts.append(jnp.dot(m.astype(BF16), tri, preferred_element_type=F32) + carry)
        carry = carry + jnp.sum(m, axis=-1, keepdims=True)
    return parts[0] if len(parts) == 1 else jnp.concatenate(parts, axis=-1)


def _route_kernel(aff_ref, idx_ref, gate_ref, *, n, cap, n_exp):
    a = aff_ref[0]
    bits = pltpu.bitcast(a, I32)

    def bit_step(b, thr):
        cand = thr | jnp.left_shift(jnp.int32(1), 30 - b)
        cnt = jnp.sum(jnp.where(bits >= cand, 1.0, 0.0), axis=-1, keepdims=True)
        return jnp.where(cnt >= cap, cand, thr)

    thr = lax.fori_loop(0, 31, bit_step, jnp.zeros((n_exp, 1), I32))
    gt = bits > thr
    eq = bits == thr
    cw = min(256, n)
    tri = jnp.where(lax.broadcasted_iota(I32, (cw, cw), 0) < lax.broadcasted_iota(I32, (cw, cw), 1),
                    1.0, 0.0).astype(BF16)
    need = cap - jnp.sum(jnp.where(gt, 1.0, 0.0), axis=-1, keepdims=True)
    sel = gt | (eq & (_prefix_excl(jnp.where(eq, 1.0, 0.0), tri) < need))
    pos = jnp.where(sel, _prefix_excl(jnp.where(sel, 1.0, 0.0), tri), -1.0)

    tok = lax.broadcasted_iota(I32, (1, n), 1)
    t_hi = (tok // 64).astype(F32)
    t_lo = (tok % 64).astype(F32)
    g_hi = a.astype(BF16).astype(F32)
    r1 = a - g_hi
    g_mid = r1.astype(BF16).astype(F32)
    g_lo = r1 - g_mid
    slot = lax.broadcasted_iota(I32, (cap, n), 0).astype(F32)
    row = lax.broadcasted_iota(I32, (16, n), 0)
    for e in range(n_exp):
        onehot = jnp.where(pos[e:e + 1, :] == slot, 1.0, 0.0).astype(BF16)
        lhs = jnp.where(row == 0, t_hi,
              jnp.where(row == 1, t_lo,
              jnp.where(row == 2, g_hi[e:e + 1, :],
              jnp.where(row == 3, g_mid[e:e + 1, :],
              jnp.where(row == 4, g_lo[e:e + 1, :], 0.0))))).astype(BF16)
        res = _dot_nt(lhs, onehot)
        idx_ref[0, e:e + 1, :] = (res[0:1, :] * 64.0 + res[1:2, :]).astype(I32)
        gate_ref[0, e:e + 1, :] = res[2:3, :] + res[3:4, :] + res[4:5, :]


def _route(aff_sets):
    n_sets, n_exp, n = aff_sets.shape
    cap = EC_CAPACITY * n // n_exp
    spec = pl.BlockSpec((1, n_exp, cap), lambda s: (s, 0, 0))
    return pl.pallas_call(
        functools.partial(_route_kernel, n=n, cap=cap, n_exp=n_exp),
        grid=(n_sets,),
        in_specs=[pl.BlockSpec((1, n_exp, n), lambda s: (s, 0, 0))],
        out_specs=[spec, spec],
        out_shape=[jax.ShapeDtypeStruct((n_sets, n_exp, cap), I32),
                   jax.ShapeDtypeStruct((n_sets, n_exp, cap), F32)],
        compiler_params=_params(("arbitrary",)),
        name="route",
    )(aff_sets)


def _ffn_kernel(idx_ref, hpk_hbm, gate_ref, g2_ref, wg_ref, wu_ref, wd_ref, x_in_hbm, x_hbm,
                hpk, hb, yacc, xbuf, sem_h, sem_x, sem_s, *, m, n_f, n_groups, groups, n_split, per, ctx_seg):
    del x_in_hbm
    e = pl.program_id(0)
    part = pl.program_id(1)
    f = pl.program_id(2)
    g = e * n_split + part
    slot = g % 2
    g_prev = jnp.maximum(g - 1, 0)
    g_next = (g + 1) % n_groups
    half = n_f // 2
    rows_rw = m // half
    rows_h = m // n_f
    d_half = hpk.shape[2]

    def h_copy(r, j, dst_slot):
        return pltpu.make_async_copy(hpk_hbm.at[pl.ds(r, 1), :], hpk.at[dst_slot, pl.ds(j, 1), :],
                                     sem_h.at[dst_slot])

    def x_copy(r, j):
        return pltpu.make_async_copy(x_hbm.at[pl.ds(r, 1), :], xbuf.at[pl.ds(j, 1), :], sem_x)

    def s_copy(r, j):
        return pltpu.make_async_copy(xbuf.at[pl.ds(j, 1), :], x_hbm.at[pl.ds(r, 1), :], sem_s)

    def start_rows(copy, grp, first, n):
        for j in range(n):
            copy(idx_ref[grp * m + first + j], first + j).start()

    def wait_h(s):
        pltpu.make_async_copy(hpk.at[1 - s], hpk.at[s], sem_h.at[s]).wait()

    def wait_rows(sem):
        pltpu.make_async_copy(yacc, xbuf, sem).wait()

    def for_rows(fn):
        def body(j, carry):
            fn(j)
            return carry
        lax.fori_loop(0, m, body, 0)

    @pl.when((g == 0) & (f == 0))
    def _():
        for_rows(lambda j: h_copy(idx_ref[j], j, 0).start())
        for_rows(lambda j: x_copy(idx_ref[j], j).start())
        wait_rows(sem_x)

    @pl.when(f == 0)
    def _():
        wait_h(slot)
        p = hpk[slot]
        hb[:, :d_half] = lax.bitcast_convert_type(p << 16, F32).astype(BF16)
        hb[:, d_half:] = lax.bitcast_convert_type(p & jnp.uint32(0xFFFF0000), F32).astype(BF16)
        yacc[...] = jnp.zeros_like(yacc)

    @pl.when(f == half)
    def _():
        wait_rows(sem_s)

    def matmuls():
        h = hb[...]
        a = jnp.dot(h, wg_ref[...].astype(BF16), preferred_element_type=F32)
        u = jnp.dot(h, wu_ref[...].astype(BF16), preferred_element_type=F32)
        z = (a / (1.0 + jnp.exp(-a)) * u).astype(BF16)
        yacc[...] += jnp.dot(z, wd_ref[...].astype(BF16), preferred_element_type=F32)

    for step in range(n_f):
        @pl.when(f == step)
        def _(step=step):
            if step < half:
                start_rows(s_copy, g_prev, step * rows_rw, rows_rw)
            else:
                start_rows(x_copy, g, (step - half) * rows_rw, rows_rw)
            start_rows(lambda r, j: h_copy(r, j, 1 - slot), g_next, step * rows_h, rows_h)
            matmuls()

    @pl.when(f == n_f - 1)
    def _():
        wait_rows(sem_x)
        for off, size, sample in groups:
            seg = ctx_seg if sample is None else part * per + sample
            rows = pl.ds(off, size)
            xbuf[rows, :] = xbuf[rows, :] + g2_ref[pl.ds(seg, 1), :] * (gate_ref[0, 0, rows, :] * yacc[rows, :])

    @pl.when((g == n_groups - 1) & (f == n_f - 1))
    def _():
        for_rows(lambda j: s_copy(idx_ref[g * m + j], j).start())
        wait_rows(sem_s)
        wait_h(1 - slot)


def _experts(idx_flat, gate, hpk, g2tab, w_gate, w_up, w_down, xs, layer, m, groups, n_split, per, ctx_seg):
    d = xs.shape[1]
    n_exp, d_exp = w_gate.shape[1], w_gate.shape[3]
    tf = _pick_tile(256, d_exp // 2)
    n_f = d_exp // tf
    n_groups = n_exp * n_split
    assert n_f % 2 == 0 and m % n_f == 0 and n_groups % 2 == 0
    grid_spec = pltpu.PrefetchScalarGridSpec(
        num_scalar_prefetch=1,
        grid=(n_exp, n_split, n_f),
        in_specs=[
            pl.BlockSpec(memory_space=pl.ANY),
            pl.BlockSpec((1, 1, m, 1), lambda e, p, f, idx: (e, p, 0, 0)),
            pl.BlockSpec(g2tab.shape, lambda e, p, f, idx: (0, 0)),
            pl.BlockSpec((None, None, d, tf), lambda e, p, f, idx: (layer, e, 0, f)),
            pl.BlockSpec((None, None, d, tf), lambda e, p, f, idx: (layer, e, 0, f)),
            pl.BlockSpec((None, None, tf, d), lambda e, p, f, idx: (layer, e, f, 0)),
            pl.BlockSpec(memory_space=pl.ANY),
        ],
        out_specs=pl.BlockSpec(memory_space=pl.ANY),
        scratch_shapes=[pltpu.VMEM((2, m, d // 2), U32), pltpu.VMEM((m, d), BF16), pltpu.VMEM((m, d), F32),
                        pltpu.VMEM((m, d), F32), pltpu.SemaphoreType.DMA((2,)), pltpu.SemaphoreType.DMA,
                        pltpu.SemaphoreType.DMA],
    )
    return pl.pallas_call(
        functools.partial(_ffn_kernel, m=m, n_f=n_f, n_groups=n_groups, groups=tuple(groups), n_split=n_split,
                          per=per, ctx_seg=ctx_seg),
        grid_spec=grid_spec,
        out_shape=jax.ShapeDtypeStruct(xs.shape, xs.dtype),
        input_output_aliases={7: 0},
        compiler_params=_params(("arbitrary",) * 3),
        name="experts",
    )(idx_flat, hpk, gate, g2tab, w_gate, w_up, w_down, xs)


def _rope_tables(seq):
    pos = np.arange(seq)
    n_freq = HEAD_DIM // 4
    inv = jnp.asarray(ROPE_THETA, F32) ** (-jnp.arange(n_freq, dtype=F32) / n_freq)
    row = jnp.asarray(pos // GRID_W, F32)[:, None] * inv
    col = jnp.asarray(pos % GRID_W, F32)[:, None] * inv
    zero = jnp.zeros_like(row)
    cos = jnp.concatenate([jnp.cos(row), jnp.cos(row), jnp.cos(col), jnp.cos(col)], axis=-1)
    s1 = jnp.concatenate([zero, jnp.sin(row), zero, jnp.sin(col)], axis=-1)
    s2 = jnp.concatenate([-jnp.sin(row), zero, -jnp.sin(col), zero], axis=-1)

    def padded(tm):
        ident = jnp.ones((tm, HEAD_DIM), F32)
        none = jnp.zeros((tm, HEAD_DIM), F32)
        return (jnp.concatenate([cos, ident], axis=0), jnp.concatenate([s1, none], axis=0),
                jnp.concatenate([s2, none], axis=0))

    return padded


def _expert_rows(idx_l, gate_l, idx_c, gate_c, dims, n_split):
    n_batch, seq, ctx_len, _ = dims
    per = n_batch // n_split
    n_exp = idx_l.shape[1]

    def arrange(v):
        s = v.shape[-1]
        return v.reshape(n_split, per, n_exp, s).transpose(2, 0, 1, 3).reshape(n_exp, n_split, per * s)

    rows_l = idx_l + (jnp.arange(n_batch, dtype=I32) * seq)[:, None, None]
    parts_i, parts_g = [arrange(rows_l)], [arrange(gate_l)]
    groups = [(b * idx_l.shape[-1], idx_l.shape[-1], b) for b in range(per)]
    if idx_c is not None:
        rows_c = idx_c + (n_batch * seq + jnp.arange(n_batch, dtype=I32) * ctx_len)[:, None, None]
        parts_i.append(arrange(rows_c))
        parts_g.append(arrange(gate_c))
        groups += [(per * idx_l.shape[-1], per * idx_c.shape[-1], None)]
    idx = jnp.concatenate(parts_i, axis=-1)
    gate = jnp.concatenate(parts_g, axis=-1)
    return idx, gate, groups


def kernel(x, c, ctx, c_ctx, w_mod, b_mod, norm1_g, norm2_g, w_in, qn_a, kn_a, sink_a, vn_b, w_s, b_s,
           qn_c, kn_c, out_g, w_out, w_router, w_gate, w_up, w_down):
    n_batch, seq, d = x.shape
    ctx_len = ctx.shape[1]
    depth = w_mod.shape[0]
    n_exp = w_router.shape[2]
    dims = (n_batch, seq, ctx_len, d)
    t_lat = n_batch * seq
    assert n_batch < MOD_ROWS and seq % WINDOW == 0 and ctx_len % B_CHUNK == 0 and t_lat % ctx_len == 0
    n_split = 2 if n_batch % 2 == 0 else 1
    per = n_batch // n_split

    xs = jnp.concatenate([x.reshape(t_lat, d), ctx.reshape(n_batch * ctx_len, d)], axis=0)
    cvec = jnp.zeros((MOD_ROWS, d), F32).at[:n_batch].set(c).at[n_batch].set(c_ctx)
    rope = _rope_tables(seq)

    for layer in range(depth):
        last = layer == depth - 1
        mod = _modulation(cvec, w_mod, b_mod, layer)
        modt = mod.reshape(MOD_ROWS * N_MOD, 1, d)
        g2tab = mod.reshape(MOD_ROWS, N_MOD, d)[:, N_MOD - 1, :]
        row = lambda v: v[layer].reshape(1, -1)
        bsb = jnp.broadcast_to(b_s[layer][:, :, None], (B_GROUPS, B_CHUNK, HEAD_DIM))
        qa, ka, va, qc, kc, vc, o_b = _inproj(
            xs, modt, row(norm1_g), w_in[layer].astype(BF16), rope, row(qn_a), row(kn_a), row(qn_c),
            row(kn_c), row(vn_b), w_s[layer].astype(BF16), bsb, dims)
        o_a = _attn_win(sink_a[layer], qa, ka, va, dims)
        o_c = _attn_glob(qc, kc, vc, dims)
        o_ax = o_cx = None
        if not last:
            o_ax = _attn_ctx(sink_a[layer], qa, ka, va, A_HEADS, A_KV, True, dims)
            o_cx = _attn_ctx(sink_a[layer], qc, kc, vc, C_HEADS, C_KV, False, dims)
        rows_out = t_lat if last else xs.shape[0]
        xs, h2, aff = _merge(o_a, o_c, o_ax, o_cx, o_b, row(out_g), w_out[layer].astype(BF16), xs, modt,
                             row(norm2_g), w_router[layer], rows_out, dims)

        idx_l, gate_l = _route(aff[:t_lat].reshape(n_batch, seq, n_exp).transpose(0, 2, 1))
        idx_c = gate_c = None
        if not last:
            idx_c, gate_c = _route(aff[t_lat:].reshape(n_batch, ctx_len, n_exp).transpose(0, 2, 1))
        idx, gate, groups = _expert_rows(idx_l, gate_l, idx_c, gate_c, dims, n_split)
        m = idx.shape[-1]
        xs = _experts(idx.reshape(-1), gate.reshape(n_exp, n_split, m, 1), h2, g2tab, w_gate, w_up, w_down,
                      xs, layer, m, groups, n_split, per, n_batch)
    return xs.reshape(n_batch, seq, d)
```

```python
import functools

import jax
import jax.numpy as jnp
import numpy as np
from jax import lax
from jax.experimental import pallas as pl
from jax.experimental.pallas import tpu as pltpu

F32 = jnp.float32
BF16 = jnp.bfloat16
I32 = jnp.int32
U32 = jnp.uint32

HEAD_DIM = 128
GRID_W = 64
WINDOW = 128
A_HEADS, A_KV = 6, 2
B_GROUPS, B_CHUNK = 4, 128
C_HEADS, C_KV = 6, 2
A_WIDTH = A_HEADS * HEAD_DIM
B_WIDTH = B_GROUPS * HEAD_DIM
C_WIDTH = C_HEADS * HEAD_DIM
KV_WIDTH = A_KV * HEAD_DIM
OFF_QA = 0
OFF_KA = OFF_QA + A_WIDTH
OFF_VA = OFF_KA + KV_WIDTH
OFF_UB = OFF_VA + KV_WIDTH
OFF_VB = OFF_UB + B_WIDTH
OFF_QC = OFF_VB + B_WIDTH
OFF_KC = OFF_QC + C_WIDTH
OFF_VC = OFF_KC + KV_WIDTH
IN_WIDTH = OFF_VC + KV_WIDTH
EC_CAPACITY = 2
N_MOD = 6
ROPE_THETA = 10000.0
EPS = 1e-6
NEG_INF = -1e30
LOG2E = 1.4426950408889634
QK_SCALE = HEAD_DIM ** -0.5 * LOG2E
MOD_ROWS = 8

VMEM_LIMIT = 56 * 1024 * 1024

def _params(sem, vmem=VMEM_LIMIT):
    return pltpu.CompilerParams(dimension_semantics=sem, vmem_limit_bytes=vmem)


def _pick_tile(target, *sizes):
    t = target
    while any(s % t for s in sizes):
        t //= 2
    return t


def _rms_scale(x):
    return x * lax.rsqrt(jnp.mean(x * x, axis=-1, keepdims=True) + EPS)


def _dot_nt(a, b):
    return lax.dot_general(a, b, (((1,), (1,)), ((), ())), preferred_element_type=F32)


def _mod_kernel(c_ref, w_ref, b_ref, o_ref):
    c = c_ref[...]
    s = c / (1.0 + jnp.exp(-c))
    o_ref[...] = jnp.dot(s, w_ref[...], precision=lax.Precision.HIGHEST,
                         preferred_element_type=F32) + b_ref[...]


def _modulation(cvec, w_mod, b_mod, layer):
    d = cvec.shape[1]
    n = w_mod.shape[2]
    tn = _pick_tile(1024, n)
    return pl.pallas_call(
        _mod_kernel,
        grid=(n // tn,),
        in_specs=[
            pl.BlockSpec((MOD_ROWS, d), lambda j: (0, 0)),
            pl.BlockSpec((None, d, tn), lambda j: (layer, 0, j)),
            pl.BlockSpec((None, 1, tn), lambda j: (layer, 0, j)),
        ],
        out_specs=pl.BlockSpec((MOD_ROWS, tn), lambda j: (0, j)),
        out_shape=jax.ShapeDtypeStruct((MOD_ROWS, n), F32),
        compiler_params=_params(("arbitrary",)),
        name="modulation",
    )(cvec, w_mod, b_mod.reshape(b_mod.shape[0], 1, n))


def _inproj_kernel(x_ref, sh_ref, sc_ref, g_ref, w_ref, cos_ref, s1_ref, s2_ref,
                   qna_ref, kna_ref, qnc_ref, knc_ref, vnb_ref, ws_ref, bsb_ref,
                   qa_ref, ka_ref, va_ref, qc_ref, kc_ref, vc_ref, ob_ref, *, tm):
    x = x_ref[...]
    h = _rms_scale(x) * g_ref[...]
    h = h * (1.0 + sc_ref[0]) + sh_ref[0]
    hb = h.astype(BF16)
    cos, s1, s2 = cos_ref[...], s1_ref[...], s2_ref[...]

    def proj(off, width):
        return jnp.dot(hb, w_ref[:, off:off + width], preferred_element_type=F32)

    def qk_head(p, gain, scale):
        q = _rms_scale(p) * gain
        q = q * cos + pltpu.roll(q, 32, 1) * s1 + pltpu.roll(q, 96, 1) * s2
        if scale != 1.0:
            q = q * scale
        return q.astype(BF16)

    def heads(off, n_heads, gain_ref, scale, out_ref):
        p = proj(off, n_heads * HEAD_DIM)
        for hd in range(n_heads):
            sl = slice(hd * HEAD_DIM, (hd + 1) * HEAD_DIM)
            out_ref[:, sl] = qk_head(p[:, sl], gain_ref[...], scale)

    heads(OFF_QA, A_HEADS, qna_ref, QK_SCALE, qa_ref)
    heads(OFF_KA, A_KV, kna_ref, 1.0, ka_ref)
    va_ref[...] = proj(OFF_VA, KV_WIDTH).astype(BF16)
    heads(OFF_QC, C_HEADS, qnc_ref, QK_SCALE, qc_ref)
    heads(OFF_KC, C_KV, knc_ref, 1.0, kc_ref)
    vc_ref[...] = proj(OFF_VC, KV_WIDTH).astype(BF16)

    pu = proj(OFF_UB, B_WIDTH)
    pv = proj(OFF_VB, B_WIDTH)
    for g in range(B_GROUPS):
        sl = slice(g * HEAD_DIM, (g + 1) * HEAD_DIM)
        u = jax.nn.gelu(pu[:, sl])
        vn = (_rms_scale(jax.nn.gelu(pv[:, sl])) * vnb_ref[:, sl]).astype(BF16)
        for c in range(tm // B_CHUNK):
            rows = slice(c * B_CHUNK, (c + 1) * B_CHUNK)
            mixed = jnp.dot(ws_ref[g], vn[rows, :], preferred_element_type=F32) + bsb_ref[g]
            ob_ref[rows, sl] = u[rows, :] * mixed


def _row_seg(i, n_lat_tiles, tiles_per_seq, n_batch):
    return jnp.where(i < n_lat_tiles, i // tiles_per_seq, n_batch)


def _inproj(xs, modt, norm_g, w_in_bf, rope, qn_a, kn_a, qn_c, kn_c, vn_b, ws_bf, bsb, dims):
    n_batch, seq, ctx_len, d = dims
    t_rows = xs.shape[0]
    tm = _pick_tile(512, seq, n_batch * ctx_len)
    n_lat_tiles = n_batch * seq // tm
    tps = seq // tm
    cos_t, s1_t, s2_t = rope(tm)

    def mod_spec(k):
        return pl.BlockSpec((1, 1, d), lambda i: (_row_seg(i, n_lat_tiles, tps, n_batch) * N_MOD + k, 0, 0))

    def rope_spec():
        return pl.BlockSpec((tm, HEAD_DIM), lambda i: (jnp.where(i < n_lat_tiles, i % tps, tps), 0))

    def full(shape):
        return pl.BlockSpec(shape, lambda i: (0,) * len(shape))

    def out(width, dtype):
        return pl.BlockSpec((tm, width), lambda i: (i, 0)), jax.ShapeDtypeStruct((t_rows, width), dtype)

    outs = [out(A_WIDTH, BF16), out(KV_WIDTH, BF16), out(KV_WIDTH, BF16),
            out(C_WIDTH, BF16), out(KV_WIDTH, BF16), out(KV_WIDTH, BF16), out(B_WIDTH, F32)]
    return pl.pallas_call(
        functools.partial(_inproj_kernel, tm=tm),
        grid=(t_rows // tm,),
        in_specs=[
            pl.BlockSpec((tm, d), lambda i: (i, 0)),
            mod_spec(0), mod_spec(1),
            full((1, d)),
            full((d, IN_WIDTH)),
            rope_spec(), rope_spec(), rope_spec(),
            full((1, HEAD_DIM)), full((1, HEAD_DIM)), full((1, HEAD_DIM)), full((1, HEAD_DIM)),
            full((1, B_WIDTH)),
            full((B_GROUPS, B_CHUNK, B_CHUNK)),
            full((B_GROUPS, B_CHUNK, HEAD_DIM)),
        ],
        out_specs=[o[0] for o in outs],
        out_shape=[o[1] for o in outs],
        compiler_params=_params(("arbitrary",)),
        name="inproj",
    )(xs, modt, modt, norm_g, w_in_bf, cos_t, s1_t, s2_t, qn_a, kn_a, qn_c, kn_c, vn_b, ws_bf, bsb)


def _softmax_pv(s_list, v_list, sink):
    m = s_list[0].max(axis=-1, keepdims=True)
    for s in s_list[1:]:
        m = jnp.maximum(m, s.max(axis=-1, keepdims=True))
    if sink is not None:
        m = jnp.maximum(m, sink)
    l = None
    o = None
    for s, v in zip(s_list, v_list):
        p = jnp.exp2(s - m)
        ls = jnp.sum(p, axis=-1, keepdims=True)
        os_ = jnp.dot(p.astype(BF16), v, preferred_element_type=F32)
        l = ls if l is None else l + ls
        o = os_ if o is None else o + os_
    if sink is not None:
        l = l + jnp.exp2(sink - m)
    return o / l


def _attn_win_kernel(sink_ref, q_ref, kx_ref, kp_ref, km_ref, kn_ref, vx_ref, vp_ref, vm_ref, vn_ref,
                     o_ref, *, tq, seq, n_g):
    i = pl.program_id(1)
    kvh = pl.program_id(2)
    kband = jnp.concatenate([kp_ref[...], km_ref[...], kn_ref[...]], axis=0)
    vband = jnp.concatenate([vp_ref[...], vm_ref[...], vn_ref[...]], axis=0)
    kx, vx = kx_ref[...], vx_ref[...]
    nb = 3 * WINDOW
    rows_all = n_g * WINDOW
    r = lax.broadcasted_iota(I32, (rows_all, nb), 0) % WINDOW
    c = lax.broadcasted_iota(I32, (rows_all, nb), 1)
    head = lax.broadcasted_iota(I32, (rows_all, 1), 0) // WINDOW
    sink = sink_ref[kvh * n_g] * LOG2E
    for g in range(1, n_g):
        sink = jnp.where(head == g, sink_ref[kvh * n_g + g] * LOG2E, sink)

    def scores(j):
        rows = slice(j * WINDOW, (j + 1) * WINDOW)
        q = jnp.concatenate([q_ref[rows, g * HEAD_DIM:(g + 1) * HEAD_DIM] for g in range(n_g)], axis=0)
        kpos = c + (i * tq + (j - 1) * WINDOW)
        valid = (jnp.abs(c - WINDOW - r) <= WINDOW) & (kpos >= 0) & (kpos < seq)
        s_band = jnp.where(valid, _dot_nt(q, kband[j * WINDOW:j * WINDOW + nb, :]), NEG_INF)
        return [_dot_nt(q, kx), s_band]

    n_blocks = tq // WINDOW
    s_next = scores(0)
    for j in range(n_blocks):
        s_cur = s_next
        if j + 1 < n_blocks:
            s_next = scores(j + 1)
        o = _softmax_pv(s_cur, [vx, vband[j * WINDOW:j * WINDOW + nb, :]], sink)
        for g in range(n_g):
            o_ref[j * WINDOW:(j + 1) * WINDOW, g * HEAD_DIM:(g + 1) * HEAD_DIM] = o[g * WINDOW:(g + 1) * WINDOW, :]


def _attn_win(sink, qa, ka, va, dims):
    n_batch, seq, ctx_len, _ = dims
    t_rows = n_batch * seq
    tq = _pick_tile(512, seq)
    nq = seq // tq
    n_g = A_HEADS // A_KV
    wb = tq // WINDOW
    sb = seq // WINDOW
    lat_ctx_blocks = n_batch * seq // ctx_len

    def q_map(b, i, k):
        return (b * nq + i, k)

    def prev_map(b, i, k):
        return (b * sb + jnp.maximum(i * wb - 1, 0), k)

    def next_map(b, i, k):
        return (b * sb + jnp.minimum((i + 1) * wb, sb - 1), k)

    def ctx_map(b, i, k):
        return (lat_ctx_blocks + b, k)

    kv_specs = [pl.BlockSpec((ctx_len, HEAD_DIM), ctx_map), pl.BlockSpec((WINDOW, HEAD_DIM), prev_map),
                pl.BlockSpec((tq, HEAD_DIM), q_map), pl.BlockSpec((WINDOW, HEAD_DIM), next_map)]
    return pl.pallas_call(
        functools.partial(_attn_win_kernel, tq=tq, seq=seq, n_g=n_g),
        grid=(n_batch, nq, A_KV),
        in_specs=[pl.BlockSpec(memory_space=pltpu.SMEM),
                  pl.BlockSpec((tq, n_g * HEAD_DIM), q_map)] + kv_specs + kv_specs,
        out_specs=pl.BlockSpec((tq, n_g * HEAD_DIM), q_map),
        out_shape=jax.ShapeDtypeStruct((t_rows, A_WIDTH), F32),
        compiler_params=_params(("arbitrary",) * 3),
        name="attn_window",
    )(sink, qa, ka, ka, ka, ka, va, va, va, va)


def _attn_glob_kernel(q_ref, kx_ref, kl_ref, vx_ref, vl_ref, o_ref, *, n_g, sub):
    kx, kl, vx, vl = kx_ref[...], kl_ref[...], vx_ref[...], vl_ref[...]
    chains = [(slice(r0, r0 + sub), slice(g * HEAD_DIM, (g + 1) * HEAD_DIM))
              for r0 in range(0, q_ref.shape[0], sub) for g in range(n_g)]

    def scores(chain):
        q = q_ref[chain[0], chain[1]]
        return [_dot_nt(q, kx), _dot_nt(q, kl)]

    s_next = scores(chains[0])
    for i, chain in enumerate(chains):
        s_cur = s_next
        if i + 1 < len(chains):
            s_next = scores(chains[i + 1])
        o_ref[chain[0], chain[1]] = _softmax_pv(s_cur, [vx, vl], None)


def _attn_glob(qc, kc, vc, dims):
    n_batch, seq, ctx_len, _ = dims
    t_rows = n_batch * seq
    tq = _pick_tile(512, seq)
    sub = min(256, tq)
    nq = seq // tq
    n_g = C_HEADS // C_KV
    lat_ctx_blocks = n_batch * seq // ctx_len

    def q_map(b, k, i):
        return (b * nq + i, k)

    kv_specs = [pl.BlockSpec((ctx_len, HEAD_DIM), lambda b, k, i: (lat_ctx_blocks + b, k)),
                pl.BlockSpec((seq, HEAD_DIM), lambda b, k, i: (b, k))]
    return pl.pallas_call(
        functools.partial(_attn_glob_kernel, n_g=n_g, sub=sub),
        grid=(n_batch, C_KV, nq),
        in_specs=[pl.BlockSpec((tq, n_g * HEAD_DIM), q_map)] + kv_specs + kv_specs,
        out_specs=pl.BlockSpec((tq, n_g * HEAD_DIM), q_map),
        out_shape=jax.ShapeDtypeStruct((t_rows, C_WIDTH), F32),
        compiler_params=_params(("arbitrary",) * 3),
        name="attn_global",
    )(qc, kc, kc, vc, vc)


def _attn_ctx_kernel(sink_ref, q_ref, k_ref, v_ref, o_ref, *, n_g, use_sink):
    kvh = pl.program_id(1)
    k, v = k_ref[...], v_ref[...]
    for g in range(n_g):
        sl = slice(g * HEAD_DIM, (g + 1) * HEAD_DIM)
        sink = sink_ref[kvh * n_g + g] * LOG2E if use_sink else None
        o_ref[:, sl] = _softmax_pv([_dot_nt(q_ref[:, sl], k)], [v], sink)


def _attn_ctx(sink, q, k, v, n_heads, n_kv, use_sink, dims):
    n_batch, seq, ctx_len, _ = dims
    n_g = n_heads // n_kv
    lat_ctx_blocks = n_batch * seq // ctx_len

    def blk(b, k):
        return (lat_ctx_blocks + b, k)

    return pl.pallas_call(
        functools.partial(_attn_ctx_kernel, n_g=n_g, use_sink=use_sink),
        grid=(n_batch, n_kv),
        in_specs=[pl.BlockSpec(memory_space=pltpu.SMEM),
                  pl.BlockSpec((ctx_len, n_g * HEAD_DIM), blk),
                  pl.BlockSpec((ctx_len, HEAD_DIM), blk),
                  pl.BlockSpec((ctx_len, HEAD_DIM), blk)],
        out_specs=pl.BlockSpec((ctx_len, n_g * HEAD_DIM), lambda b, k: (b, k)),
        out_shape=jax.ShapeDtypeStruct((n_batch * ctx_len, n_heads * HEAD_DIM), F32),
        compiler_params=_params(("arbitrary",) * 2),
        name="attn_context",
    )(sink, q, k, v)


def _merge_kernel(*refs, n_lat_tiles, has_ctx, sub):
    if has_ctx:
        oa_ref, oc_ref, oax_ref, ocx_ref = refs[:4]
        refs = refs[4:]
    else:
        oa_ref, oc_ref = refs[:2]
        refs = refs[2:]
    ob_ref, og_ref, w_ref, x_ref, g1_ref, sh_ref, sc_ref, n2_ref, wr_ref, xo_ref, h_ref, aff_ref = refs
    is_lat = pl.program_id(0) < n_lat_tiles
    n_exp = aff_ref.shape[1]
    half = x_ref.shape[1] // 2

    def group(o, off, width):
        y = (_rms_scale(o) * og_ref[:, off:off + width]).astype(BF16)
        return jnp.dot(y, w_ref[off:off + width, :], preferred_element_type=F32)

    for r0 in range(0, x_ref.shape[0], sub):
        rows = slice(r0, r0 + sub)
        o_a, o_c = oa_ref[rows, :], oc_ref[rows, :]
        if has_ctx:
            o_a = jnp.where(is_lat, o_a, oax_ref[rows, :])
            o_c = jnp.where(is_lat, o_c, ocx_ref[rows, :])
        y = (group(o_a, 0, A_WIDTH) + group(ob_ref[rows, :], A_WIDTH, B_WIDTH)
             + group(o_c, A_WIDTH + B_WIDTH, C_WIDTH))
        xn = x_ref[rows, :] + g1_ref[0] * y
        xo_ref[rows, :] = xn
        h = _rms_scale(xn) * n2_ref[...]
        h = h * (1.0 + sc_ref[0]) + sh_ref[0]
        hb = h.astype(BF16)
        lo = lax.bitcast_convert_type(hb[:, :half].astype(F32), U32) >> 16
        hi = lax.bitcast_convert_type(hb[:, half:].astype(F32), U32) & jnp.uint32(0xFFFF0000)
        h_ref[rows, :] = lo | hi
        h_lo = (h - hb.astype(F32)).astype(BF16)
        both = jnp.dot(hb, wr_ref[...], preferred_element_type=F32)
        logits = (both[:, :n_exp] + both[:, n_exp:]
                  + jnp.dot(h_lo, wr_ref[:, :n_exp], preferred_element_type=F32))
        e = jnp.exp(logits - logits.max(axis=-1, keepdims=True))
        aff_ref[rows, :] = e / jnp.sum(e, axis=-1, keepdims=True)


def _merge(o_a, o_c, o_ax, o_cx, o_b, out_g, w_out_bf, xs, modt, norm2_g, w_router, rows_out, dims):
    n_batch, seq, ctx_len, d = dims
    tm = _pick_tile(512, seq, n_batch * ctx_len)
    n_lat_tiles = n_batch * seq // tm
    tps = seq // tm
    n_exp = w_router.shape[1]
    mix = A_WIDTH + B_WIDTH + C_WIDTH
    has_ctx = o_ax is not None
    assert has_ctx == (rows_out > n_batch * seq)
    w_hi = w_router.astype(BF16)
    w_lo = (w_router - w_hi.astype(F32)).astype(BF16)

    def mod_spec(k):
        return pl.BlockSpec((1, 1, d), lambda i: (_row_seg(i, n_lat_tiles, tps, n_batch) * N_MOD + k, 0, 0))

    def rows(width):
        return pl.BlockSpec((tm, width), lambda i: (i, 0))

    def lat_rows(width):
        return pl.BlockSpec((tm, width), lambda i: (jnp.minimum(i, n_lat_tiles - 1), 0))

    def ctx_rows(width):
        return pl.BlockSpec((tm, width), lambda i: (jnp.maximum(i - n_lat_tiles, 0), 0))

    def full(shape):
        return pl.BlockSpec(shape, lambda i: (0,) * len(shape))

    attn_specs = [lat_rows(A_WIDTH), lat_rows(C_WIDTH)]
    attn_args = [o_a, o_c]
    if has_ctx:
        attn_specs += [ctx_rows(A_WIDTH), ctx_rows(C_WIDTH)]
        attn_args += [o_ax, o_cx]
    return pl.pallas_call(
        functools.partial(_merge_kernel, n_lat_tiles=n_lat_tiles, has_ctx=has_ctx, sub=min(256, tm)),
        grid=(rows_out // tm,),
        in_specs=attn_specs + [rows(B_WIDTH), full((1, mix)), full((mix, d)), rows(d),
                               mod_spec(2), mod_spec(3), mod_spec(4), full((1, d)), full((d, 2 * n_exp))],
        out_specs=[rows(d), rows(d // 2), rows(n_exp)],
        out_shape=[jax.ShapeDtypeStruct((rows_out, d), F32), jax.ShapeDtypeStruct((rows_out, d // 2), U32),
                   jax.ShapeDtypeStruct((rows_out, n_exp), F32)],
        compiler_params=_params(("arbitrary",)),
        name="merge_router",
    )(*attn_args, o_b, out_g, w_out_bf, xs, modt, modt, modt, norm2_g, jnp.concatenate([w_hi, w_lo], axis=1))


def _prefix_excl(mask_f32, tri):
    rows, n = mask_f32.shape
    cw = tri.shape[0]
    carry = jnp.zeros((rows, 1), F32)
    parts = []
    for c in range(n // cw):
        m = mask_f32[:, c * cw:(c + 1) * cw]
        parts.append(jnp.dot(m.astype(BF16), tri, preferred_element_type=F32) + carry)
        carry = carry + jnp.sum(m, axis=-1, keepdims=True)
    return parts[0] if len(parts) == 1 else jnp.concatenate(parts, axis=-1)


def _route_kernel(aff_ref, idx_ref, gate_ref, *, n, cap, n_exp):
    a = aff_ref[0]
    bits = pltpu.bitcast(a, I32)

    def bit_step(b, thr):
        cand = thr | jnp.left_shift(jnp.int32(1), 30 - b)
        cnt = jnp.sum(jnp.where(bits >= cand, 1.0, 0.0), axis=-1, keepdims=True)
        return jnp.where(cnt >= cap, cand, thr)

    thr = lax.fori_loop(0, 31, bit_step, jnp.zeros((n_exp, 1), I32))
    gt = bits > thr
    eq = bits == thr
    cw = min(256, n)
    tri = jnp.where(lax.broadcasted_iota(I32, (cw, cw), 0) < lax.broadcasted_iota(I32, (cw, cw), 1),
                    1.0, 0.0).astype(BF16)
    need = cap - jnp.sum(jnp.where(gt, 1.0, 0.0), axis=-1, keepdims=True)
    sel = gt | (eq & (_prefix_excl(jnp.where(eq, 1.0, 0.0), tri) < need))
    pos = jnp.where(sel, _prefix_excl(jnp.where(sel, 1.0, 0.0), tri), -1.0)

    tok = lax.broadcasted_iota(I32, (1, n), 1)
    t_hi = (tok // 64).astype(F32)
    t_lo = (tok % 64).astype(F32)
    g_hi = a.astype(BF16).astype(F32)
    r1 = a - g_hi
    g_mid = r1.astype(BF16).astype(F32)
    g_lo = r1 - g_mid
    slot = lax.broadcasted_iota(I32, (cap, n), 0).astype(F32)
    row = lax.broadcasted_iota(I32, (16, n), 0)
    for e in range(n_exp):
        onehot = jnp.where(pos[e:e + 1, :] == slot, 1.0, 0.0).astype(BF16)
        lhs = jnp.where(row == 0, t_hi,
              jnp.where(row == 1, t_lo,
              jnp.where(row == 2, g_hi[e:e + 1, :],
              jnp.where(row == 3, g_mid[e:e + 1, :],
              jnp.where(row == 4, g_lo[e:e + 1, :], 0.0))))).astype(BF16)
        res = _dot_nt(lhs, onehot)
        idx_ref[0, e:e + 1, :] = (res[0:1, :] * 64.0 + res[1:2, :]).astype(I32)
        gate_ref[0, e:e + 1, :] = res[2:3, :] + res[3:4, :] + res[4:5, :]


def _route(aff_sets):
    n_sets, n_exp, n = aff_sets.shape
    cap = EC_CAPACITY * n // n_exp
    spec = pl.BlockSpec((1, n_exp, cap), lambda s: (s, 0, 0))
    return pl.pallas_call(
        functools.partial(_route_kernel, n=n, cap=cap, n_exp=n_exp),
        grid=(n_sets,),
        in_specs=[pl.BlockSpec((1, n_exp, n), lambda s: (s, 0, 0))],
        out_specs=[spec, spec],
        out_shape=[jax.ShapeDtypeStruct((n_sets, n_exp, cap), I32),
                   jax.ShapeDtypeStruct((n_sets, n_exp, cap), F32)],
        compiler_params=_params(("arbitrary",)),
        name="route",
    )(aff_sets)


def _ffn_kernel(idx_ref, hpk_hbm, gate_ref, g2_ref, wg_ref, wu_ref, wd_ref, x_in_hbm, x_hbm,
                hpk, hb, yacc, xbuf, sem_h, sem_x, sem_s, *, m, n_f, n_groups, groups, n_split, per, ctx_seg):
    del x_in_hbm
    e = pl.program_id(0)
    part = pl.program_id(1)
    f = pl.program_id(2)
    g = e * n_split + part
    slot = g % 2
    g_prev = jnp.maximum(g - 1, 0)
    g_next = (g + 1) % n_groups
    half = n_f // 2
    rows_rw = m // half
    rows_h = m // n_f
    d_half = hpk.shape[2]

    def h_copy(r, j, dst_slot):
        return pltpu.make_async_copy(hpk_hbm.at[pl.ds(r, 1), :], hpk.at[dst_slot, pl.ds(j, 1), :],
                                     sem_h.at[dst_slot])

    def x_copy(r, j):
        return pltpu.make_async_copy(x_hbm.at[pl.ds(r, 1), :], xbuf.at[pl.ds(j, 1), :], sem_x)

    def s_copy(r, j):
        return pltpu.make_async_copy(xbuf.at[pl.ds(j, 1), :], x_hbm.at[pl.ds(r, 1), :], sem_s)

    def start_rows(copy, grp, first, n):
        for j in range(n):
            copy(idx_ref[grp * m + first + j], first + j).start()

    def wait_h(s):
        pltpu.make_async_copy(hpk.at[1 - s], hpk.at[s], sem_h.at[s]).wait()

    def wait_rows(sem):
        pltpu.make_async_copy(yacc, xbuf, sem).wait()

    def for_rows(fn):
        def body(j, carry):
            fn(j)
            return carry
        lax.fori_loop(0, m, body, 0)

    @pl.when((g == 0) & (f == 0))
    def _():
        for_rows(lambda j: h_copy(idx_ref[j], j, 0).start())
        for_rows(lambda j: x_copy(idx_ref[j], j).start())
        wait_rows(sem_x)

    @pl.when(f == 0)
    def _():
        wait_h(slot)
        p = hpk[slot]
        hb[:, :d_half] = lax.bitcast_convert_type(p << 16, F32).astype(BF16)
        hb[:, d_half:] = lax.bitcast_convert_type(p & jnp.uint32(0xFFFF0000), F32).astype(BF16)

    @pl.when(f == half)
    def _():
        wait_rows(sem_s)

    def matmuls(first):
        h = hb[...]
        a = jnp.dot(h, wg_ref[...].astype(BF16), preferred_element_type=F32)
        u = jnp.dot(h, wu_ref[...].astype(BF16), preferred_element_type=F32)
        z = (a / (1.0 + jnp.exp(-a)) * u).astype(BF16)
        y = jnp.dot(z, wd_ref[...].astype(BF16), preferred_element_type=F32)
        if first:
            yacc[...] = y
        else:
            yacc[...] += y

    for step in range(n_f):
        @pl.when(f == step)
        def _(step=step):
            if step < half:
                start_rows(s_copy, g_prev, step * rows_rw, rows_rw)
            else:
                start_rows(x_copy, g, (step - half) * rows_rw, rows_rw)
            start_rows(lambda r, j: h_copy(r, j, 1 - slot), g_next, step * rows_h, rows_h)
            matmuls(first=step == 0)

    @pl.when(f == n_f - 1)
    def _():
        wait_rows(sem_x)
        for off, size, sample in groups:
            seg = ctx_seg if sample is None else part * per + sample
            rows = pl.ds(off, size)
            xbuf[rows, :] = xbuf[rows, :] + g2_ref[pl.ds(seg, 1), :] * (gate_ref[0, 0, rows, :] * yacc[rows, :])

    @pl.when((g == n_groups - 1) & (f == n_f - 1))
    def _():
        for_rows(lambda j: s_copy(idx_ref[g * m + j], j).start())
        wait_rows(sem_s)
        wait_h(1 - slot)


def _experts(idx_flat, gate, hpk, g2tab, w_gate, w_up, w_down, xs, layer, m, groups, n_split, per, ctx_seg):
    d = xs.shape[1]
    n_exp, d_exp = w_gate.shape[1], w_gate.shape[3]
    tf = _pick_tile(256, d_exp // 2)
    n_f = d_exp // tf
    n_groups = n_exp * n_split
    assert n_f % 2 == 0 and m % n_f == 0 and n_groups % 2 == 0
    grid_spec = pltpu.PrefetchScalarGridSpec(
        num_scalar_prefetch=1,
        grid=(n_exp, n_split, n_f),
        in_specs=[
            pl.BlockSpec(memory_space=pl.ANY),
            pl.BlockSpec((1, 1, m, 1), lambda e, p, f, idx: (e, p, 0, 0)),
            pl.BlockSpec(g2tab.shape, lambda e, p, f, idx: (0, 0)),
            pl.BlockSpec((None, None, d, tf), lambda e, p, f, idx: (layer, e, 0, f)),
            pl.BlockSpec((None, None, d, tf), lambda e, p, f, idx: (layer, e, 0, f)),
            pl.BlockSpec((None, None, tf, d), lambda e, p, f, idx: (layer, e, f, 0)),
            pl.BlockSpec(memory_space=pl.ANY),
        ],
        out_specs=pl.BlockSpec(memory_space=pl.ANY),
        scratch_shapes=[pltpu.VMEM((2, m, d // 2), U32), pltpu.VMEM((m, d), BF16), pltpu.VMEM((m, d), F32),
                        pltpu.VMEM((m, d), F32), pltpu.SemaphoreType.DMA((2,)), pltpu.SemaphoreType.DMA,
                        pltpu.SemaphoreType.DMA],
    )
    return pl.pallas_call(
        functools.partial(_ffn_kernel, m=m, n_f=n_f, n_groups=n_groups, groups=tuple(groups), n_split=n_split,
                          per=per, ctx_seg=ctx_seg),
        grid_spec=grid_spec,
        out_shape=jax.ShapeDtypeStruct(xs.shape, xs.dtype),
        input_output_aliases={7: 0},
        compiler_params=_params(("arbitrary",) * 3),
        name="experts",
    )(idx_flat, hpk, gate, g2tab, w_gate, w_up, w_down, xs)


def _rope_tables(seq):
    pos = np.arange(seq)
    n_freq = HEAD_DIM // 4
    inv = jnp.asarray(ROPE_THETA, F32) ** (-jnp.arange(n_freq, dtype=F32) / n_freq)
    row = jnp.asarray(pos // GRID_W, F32)[:, None] * inv
    col = jnp.asarray(pos % GRID_W, F32)[:, None] * inv
    zero = jnp.zeros_like(row)
    cos = jnp.concatenate([jnp.cos(row), jnp.cos(row), jnp.cos(col), jnp.cos(col)], axis=-1)
    s1 = jnp.concatenate([zero, jnp.sin(row), zero, jnp.sin(col)], axis=-1)
    s2 = jnp.concatenate([-jnp.sin(row), zero, -jnp.sin(col), zero], axis=-1)

    def padded(tm):
        ident = jnp.ones((tm, HEAD_DIM), F32)
        none = jnp.zeros((tm, HEAD_DIM), F32)
        return (jnp.concatenate([cos, ident], axis=0), jnp.concatenate([s1, none], axis=0),
                jnp.concatenate([s2, none], axis=0))

    return padded


def _expert_rows(idx_l, gate_l, idx_c, gate_c, dims, n_split):
    n_batch, seq, ctx_len, _ = dims
    per = n_batch // n_split
    n_exp = idx_l.shape[1]

    def arrange(v):
        s = v.shape[-1]
        return v.reshape(n_split, per, n_exp, s).transpose(2, 0, 1, 3).reshape(n_exp, n_split, per * s)

    rows_l = idx_l + (jnp.arange(n_batch, dtype=I32) * seq)[:, None, None]
    parts_i, parts_g = [arrange(rows_l)], [arrange(gate_l)]
    groups = [(b * idx_l.shape[-1], idx_l.shape[-1], b) for b in range(per)]
    if idx_c is not None:
        rows_c = idx_c + (n_batch * seq + jnp.arange(n_batch, dtype=I32) * ctx_len)[:, None, None]
        parts_i.append(arrange(rows_c))
        parts_g.append(arrange(gate_c))
        groups += [(per * idx_l.shape[-1], per * idx_c.shape[-1], None)]
    idx = jnp.concatenate(parts_i, axis=-1)
    gate = jnp.concatenate(parts_g, axis=-1)
    return idx, gate, groups


def kernel(x, c, ctx, c_ctx, w_mod, b_mod, norm1_g, norm2_g, w_in, qn_a, kn_a, sink_a, vn_b, w_s, b_s,
           qn_c, kn_c, out_g, w_out, w_router, w_gate, w_up, w_down):
    n_batch, seq, d = x.shape
    ctx_len = ctx.shape[1]
    depth = w_mod.shape[0]
    n_exp = w_router.shape[2]
    dims = (n_batch, seq, ctx_len, d)
    t_lat = n_batch * seq
    assert n_batch < MOD_ROWS and seq % WINDOW == 0 and ctx_len % B_CHUNK == 0 and t_lat % ctx_len == 0
    n_split = 2 if n_batch % 2 == 0 else 1
    per = n_batch // n_split

    xs = jnp.concatenate([x.reshape(t_lat, d), ctx.reshape(n_batch * ctx_len, d)], axis=0)
    cvec = jnp.zeros((MOD_ROWS, d), F32).at[:n_batch].set(c).at[n_batch].set(c_ctx)
    rope = _rope_tables(seq)

    for layer in range(depth):
        last = layer == depth - 1
        mod = _modulation(cvec, w_mod, b_mod, layer)
        modt = mod.reshape(MOD_ROWS * N_MOD, 1, d)
        g2tab = mod.reshape(MOD_ROWS, N_MOD, d)[:, N_MOD - 1, :]
        row = lambda v: v[layer].reshape(1, -1)
        bsb = jnp.broadcast_to(b_s[layer][:, :, None], (B_GROUPS, B_CHUNK, HEAD_DIM))
        qa, ka, va, qc, kc, vc, o_b = _inproj(
            xs, modt, row(norm1_g), w_in[layer].astype(BF16), rope, row(qn_a), row(kn_a), row(qn_c),
            row(kn_c), row(vn_b), w_s[layer].astype(BF16), bsb, dims)
        o_a = _attn_win(sink_a[layer], qa, ka, va, dims)
        o_c = _attn_glob(qc, kc, vc, dims)
        o_ax = o_cx = None
        if not last:
            o_ax = _attn_ctx(sink_a[layer], qa, ka, va, A_HEADS, A_KV, True, dims)
            o_cx = _attn_ctx(sink_a[layer], qc, kc, vc, C_HEADS, C_KV, False, dims)
        rows_out = t_lat if last else xs.shape[0]
        xs, h2, aff = _merge(o_a, o_c, o_ax, o_cx, o_b, row(out_g), w_out[layer].astype(BF16), xs, modt,
                             row(norm2_g), w_router[layer], rows_out, dims)

        idx_l, gate_l = _route(aff[:t_lat].reshape(n_batch, seq, n_exp).transpose(0, 2, 1))
        idx_c = gate_c = None
        if not last:
            idx_c, gate_c = _route(aff[t_lat:].reshape(n_batch, ctx_len, n_exp).transpose(0, 2, 1))
        idx, gate, groups = _expert_rows(idx_l, gate_l, idx_c, gate_c, dims, n_split)
        m = idx.shape[-1]
        xs = _experts(idx.reshape(-1), gate.reshape(n_exp, n_split, m, 1), h2, g2tab, w_gate, w_up, w_down,
                      xs, layer, m, groups, n_split, per, n_batch)
    return xs.reshape(n_batch, seq, d)
```

```python
import functools

import jax
import jax.numpy as jnp
import numpy as np
from jax import lax
from jax.experimental import pallas as pl
from jax.experimental.pallas import tpu as pltpu

F32 = jnp.float32
BF16 = jnp.bfloat16
I32 = jnp.int32
U32 = jnp.uint32

HEAD_DIM = 128
GRID_W = 64
WINDOW = 128
A_HEADS, A_KV = 6, 2
B_GROUPS, B_CHUNK = 4, 128
C_HEADS, C_KV = 6, 2
A_WIDTH = A_HEADS * HEAD_DIM
B_WIDTH = B_GROUPS * HEAD_DIM
C_WIDTH = C_HEADS * HEAD_DIM
KV_WIDTH = A_KV * HEAD_DIM
OFF_QA = 0
OFF_KA = OFF_QA + A_WIDTH
OFF_VA = OFF_KA + KV_WIDTH
OFF_UB = OFF_VA + KV_WIDTH
OFF_VB = OFF_UB + B_WIDTH
OFF_QC = OFF_VB + B_WIDTH
OFF_KC = OFF_QC + C_WIDTH
OFF_VC = OFF_KC + KV_WIDTH
IN_WIDTH = OFF_VC + KV_WIDTH
EC_CAPACITY = 2
N_MOD = 6
ROPE_THETA = 10000.0
EPS = 1e-6
NEG_INF = -1e30
LOG2E = 1.4426950408889634
QK_SCALE = HEAD_DIM ** -0.5 * LOG2E
SUBLANES = 8
MOD_ROWS = 8

VMEM_LIMIT = 56 * 1024 * 1024

def _params(sem, vmem=VMEM_LIMIT):
    return pltpu.CompilerParams(dimension_semantics=sem, vmem_limit_bytes=vmem)


def _pick_tile(target, *sizes):
    t = target
    while any(s % t for s in sizes):
        t //= 2
    return t


def _rms_scale(x):
    return x * lax.rsqrt(jnp.mean(x * x, axis=-1, keepdims=True) + EPS)


def _dot_nt(a, b):
    return lax.dot_general(a, b, (((1,), (1,)), ((), ())), preferred_element_type=F32)


def _mod_kernel(c_ref, w_ref, b_ref, o_ref):
    c = c_ref[...]
    s = c / (1.0 + jnp.exp(-c))
    o_ref[...] = jnp.dot(s, w_ref[...], precision=lax.Precision.HIGHEST,
                         preferred_element_type=F32) + b_ref[...]


def _modulation(cvec, w_mod, b_mod, layer):
    d = cvec.shape[1]
    n = w_mod.shape[2]
    tn = _pick_tile(1024, n)
    return pl.pallas_call(
        _mod_kernel,
        grid=(n // tn,),
        in_specs=[
            pl.BlockSpec((MOD_ROWS, d), lambda j: (0, 0)),
            pl.BlockSpec((None, d, tn), lambda j: (layer, 0, j)),
            pl.BlockSpec((None, 1, tn), lambda j: (layer, 0, j)),
        ],
        out_specs=pl.BlockSpec((MOD_ROWS, tn), lambda j: (0, j)),
        out_shape=jax.ShapeDtypeStruct((MOD_ROWS, n), F32),
        compiler_params=_params(("arbitrary",)),
        name="modulation",
    )(cvec, w_mod, b_mod.reshape(b_mod.shape[0], 1, n))


def _inproj_kernel(*refs, tm, n_lat_tiles, split_input, sub):
    if split_input:
        xl_ref, xc_ref = refs[:2]
        refs = refs[2:]
    else:
        xl_ref = refs[0]
        refs = refs[1:]
    (sh_ref, sc_ref, g_ref, w_ref, cos_ref, s1_ref, s2_ref, qna_ref, kna_ref, qnc_ref, knc_ref, vnb_ref, ws_ref,
     bsb_ref, qa_ref, ka_ref, va_ref, qc_ref, kc_ref, vc_ref, ob_ref) = refs

    for r0 in range(0, tm, sub):
        rows = slice(r0, r0 + sub)
        x = xl_ref[rows, :]
        if split_input:
            x = jnp.where(pl.program_id(0) < n_lat_tiles, x, xc_ref[rows, :])
        h = _rms_scale(x) * g_ref[...]
        h = h * (1.0 + sc_ref[0]) + sh_ref[0]
        hb = h.astype(BF16)
        cos, s1, s2 = cos_ref[rows, :], s1_ref[rows, :], s2_ref[rows, :]

        def proj(off, width):
            return jnp.dot(hb, w_ref[:, off:off + width], preferred_element_type=F32)

        def qk_head(p, gain, scale):
            q = _rms_scale(p) * gain
            q = q * cos + pltpu.roll(q, 32, 1) * s1 + pltpu.roll(q, 96, 1) * s2
            if scale != 1.0:
                q = q * scale
            return q.astype(BF16)

        def heads(off, n_heads, gain_ref, scale, out_ref):
            p = proj(off, n_heads * HEAD_DIM)
            for hd in range(n_heads):
                sl = slice(hd * HEAD_DIM, (hd + 1) * HEAD_DIM)
                out_ref[rows, sl] = qk_head(p[:, sl], gain_ref[...], scale)

        pu = proj(OFF_UB, B_WIDTH)
        pv = proj(OFF_VB, B_WIDTH)
        heads(OFF_QA, A_HEADS, qna_ref, QK_SCALE, qa_ref)
        heads(OFF_KA, A_KV, kna_ref, 1.0, ka_ref)
        va_ref[rows, :] = proj(OFF_VA, KV_WIDTH).astype(BF16)
        heads(OFF_QC, C_HEADS, qnc_ref, QK_SCALE, qc_ref)
        heads(OFF_KC, C_KV, knc_ref, 1.0, kc_ref)
        vc_ref[rows, :] = proj(OFF_VC, KV_WIDTH).astype(BF16)

        for g in range(B_GROUPS):
            sl = slice(g * HEAD_DIM, (g + 1) * HEAD_DIM)
            u = jax.nn.gelu(pu[:, sl])
            vn = (_rms_scale(jax.nn.gelu(pv[:, sl])) * vnb_ref[:, sl]).astype(BF16)
            for c in range(sub // B_CHUNK):
                chunk = slice(c * B_CHUNK, (c + 1) * B_CHUNK)
                mixed = jnp.dot(ws_ref[g], vn[chunk, :], preferred_element_type=F32) + bsb_ref[g]
                ob_ref[r0 + c * B_CHUNK:r0 + (c + 1) * B_CHUNK, sl] = u[chunk, :] * mixed


def _row_seg(i, n_lat_tiles, tiles_per_seq, n_batch):
    return jnp.where(i < n_lat_tiles, i // tiles_per_seq, n_batch)


def _inproj(xs, modt, norm_g, w_in_bf, rope, qn_a, kn_a, qn_c, kn_c, vn_b, ws_bf, bsb, dims):
    n_batch, seq, ctx_len, d = dims
    split_input = isinstance(xs, tuple)
    t_rows = n_batch * (seq + ctx_len)
    tm = _pick_tile(512, seq, n_batch * ctx_len)
    n_lat_tiles = n_batch * seq // tm
    tps = seq // tm
    cos_t, s1_t, s2_t = rope(tm)
    if split_input:
        x_specs = [pl.BlockSpec((tm, d), lambda i: (jnp.minimum(i, n_lat_tiles - 1), 0)),
                   pl.BlockSpec((tm, d), lambda i: (jnp.maximum(i - n_lat_tiles, 0), 0))]
        x_args = list(xs)
    else:
        x_specs = [pl.BlockSpec((tm, d), lambda i: (i, 0))]
        x_args = [xs]

    def mod_spec(k):
        return pl.BlockSpec((1, 1, d), lambda i: (_row_seg(i, n_lat_tiles, tps, n_batch) * N_MOD + k, 0, 0))

    def rope_spec():
        return pl.BlockSpec((tm, HEAD_DIM), lambda i: (jnp.where(i < n_lat_tiles, i % tps, tps), 0))

    def full(shape):
        return pl.BlockSpec(shape, lambda i: (0,) * len(shape))

    def out(width, dtype):
        return pl.BlockSpec((tm, width), lambda i: (i, 0)), jax.ShapeDtypeStruct((t_rows, width), dtype)

    outs = [out(A_WIDTH, BF16), out(KV_WIDTH, BF16), out(KV_WIDTH, BF16),
            out(C_WIDTH, BF16), out(KV_WIDTH, BF16), out(KV_WIDTH, BF16), out(B_WIDTH, F32)]
    return pl.pallas_call(
        functools.partial(_inproj_kernel, tm=tm, n_lat_tiles=n_lat_tiles, split_input=split_input,
                          sub=min(256, tm)),
        grid=(t_rows // tm,),
        in_specs=x_specs + [
            mod_spec(0), mod_spec(1),
            full((1, d)),
            full((d, IN_WIDTH)),
            rope_spec(), rope_spec(), rope_spec(),
            full((1, HEAD_DIM)), full((1, HEAD_DIM)), full((1, HEAD_DIM)), full((1, HEAD_DIM)),
            full((1, B_WIDTH)),
            full((B_GROUPS, B_CHUNK, B_CHUNK)),
            full((B_GROUPS, B_CHUNK, HEAD_DIM)),
        ],
        out_specs=[o[0] for o in outs],
        out_shape=[o[1] for o in outs],
        compiler_params=_params(("arbitrary",)),
        name="inproj",
    )(*x_args, modt, modt, norm_g, w_in_bf, cos_t, s1_t, s2_t, qn_a, kn_a, qn_c, kn_c, vn_b, ws_bf, bsb)


def _softmax_pv(s_list, v_list, sink):
    m = s_list[0].max(axis=-1, keepdims=True)
    for s in s_list[1:]:
        m = jnp.maximum(m, s.max(axis=-1, keepdims=True))
    if sink is not None:
        m = jnp.maximum(m, sink)
    l = None
    o = None
    for s, v in zip(s_list, v_list):
        p = jnp.exp2(s - m)
        ls = jnp.sum(p, axis=-1, keepdims=True)
        os_ = jnp.dot(p.astype(BF16), v, preferred_element_type=F32)
        l = ls if l is None else l + ls
        o = os_ if o is None else o + os_
    if sink is not None:
        l = l + jnp.exp2(sink - m)
    return o / l


def _attn_win_kernel(sink_ref, q_ref, kx_ref, kp_ref, km_ref, kn_ref, vx_ref, vp_ref, vm_ref, vn_ref,
                     o_ref, *, tq, seq, n_g):
    i = pl.program_id(1)
    kvh = pl.program_id(2)
    kband = jnp.concatenate([kp_ref[...], km_ref[...], kn_ref[...]], axis=0)
    vband = jnp.concatenate([vp_ref[...], vm_ref[...], vn_ref[...]], axis=0)
    kx, vx = kx_ref[...], vx_ref[...]
    nb = 3 * WINDOW
    rows_all = n_g * WINDOW
    r = lax.broadcasted_iota(I32, (rows_all, nb), 0) % WINDOW
    c = lax.broadcasted_iota(I32, (rows_all, nb), 1)
    head = lax.broadcasted_iota(I32, (rows_all, 1), 0) // WINDOW
    sink = sink_ref[kvh * n_g] * LOG2E
    for g in range(1, n_g):
        sink = jnp.where(head == g, sink_ref[kvh * n_g + g] * LOG2E, sink)

    def scores(j):
        rows = slice(j * WINDOW, (j + 1) * WINDOW)
        q = jnp.concatenate([q_ref[rows, g * HEAD_DIM:(g + 1) * HEAD_DIM] for g in range(n_g)], axis=0)
        kpos = c + (i * tq + (j - 1) * WINDOW)
        valid = (jnp.abs(c - WINDOW - r) <= WINDOW) & (kpos >= 0) & (kpos < seq)
        s_band = jnp.where(valid, _dot_nt(q, kband[j * WINDOW:j * WINDOW + nb, :]), NEG_INF)
        return [_dot_nt(q, kx), s_band]

    n_blocks = tq // WINDOW
    s_next = scores(0)
    for j in range(n_blocks):
        s_cur = s_next
        if j + 1 < n_blocks:
            s_next = scores(j + 1)
        o = _softmax_pv(s_cur, [vx, vband[j * WINDOW:j * WINDOW + nb, :]], sink)
        for g in range(n_g):
            o_ref[j * WINDOW:(j + 1) * WINDOW, g * HEAD_DIM:(g + 1) * HEAD_DIM] = o[g * WINDOW:(g + 1) * WINDOW, :]


def _attn_win(sink, qa, ka, va, dims):
    n_batch, seq, ctx_len, _ = dims
    t_rows = n_batch * seq
    tq = _pick_tile(512, seq)
    nq = seq // tq
    n_g = A_HEADS // A_KV
    wb = tq // WINDOW
    sb = seq // WINDOW
    lat_ctx_blocks = n_batch * seq // ctx_len

    def q_map(b, i, k):
        return (b * nq + i, k)

    def prev_map(b, i, k):
        return (b * sb + jnp.maximum(i * wb - 1, 0), k)

    def next_map(b, i, k):
        return (b * sb + jnp.minimum((i + 1) * wb, sb - 1), k)

    def ctx_map(b, i, k):
        return (lat_ctx_blocks + b, k)

    kv_specs = [pl.BlockSpec((ctx_len, HEAD_DIM), ctx_map), pl.BlockSpec((WINDOW, HEAD_DIM), prev_map),
                pl.BlockSpec((tq, HEAD_DIM), q_map), pl.BlockSpec((WINDOW, HEAD_DIM), next_map)]
    return pl.pallas_call(
        functools.partial(_attn_win_kernel, tq=tq, seq=seq, n_g=n_g),
        grid=(n_batch, nq, A_KV),
        in_specs=[pl.BlockSpec(memory_space=pltpu.SMEM),
                  pl.BlockSpec((tq, n_g * HEAD_DIM), q_map)] + kv_specs + kv_specs,
        out_specs=pl.BlockSpec((tq, n_g * HEAD_DIM), q_map),
        out_shape=jax.ShapeDtypeStruct((t_rows, A_WIDTH), F32),
        compiler_params=_params(("arbitrary",) * 3),
        name="attn_window",
    )(sink, qa, ka, ka, ka, ka, va, va, va, va)


def _attn_glob_kernel(q_ref, kx_ref, kl_ref, vx_ref, vl_ref, o_ref, *, n_g, sub):
    kx, kl, vx, vl = kx_ref[...], kl_ref[...], vx_ref[...], vl_ref[...]
    chains = [(slice(r0, r0 + sub), slice(g * HEAD_DIM, (g + 1) * HEAD_DIM))
              for r0 in range(0, q_ref.shape[0], sub) for g in range(n_g)]

    def scores(chain):
        q = q_ref[chain[0], chain[1]]
        return [_dot_nt(q, kx), _dot_nt(q, kl)]

    s_next = scores(chains[0])
    for i, chain in enumerate(chains):
        s_cur = s_next
        if i + 1 < len(chains):
            s_next = scores(chains[i + 1])
        o_ref[chain[0], chain[1]] = _softmax_pv(s_cur, [vx, vl], None)


def _attn_glob(qc, kc, vc, dims):
    n_batch, seq, ctx_len, _ = dims
    t_rows = n_batch * seq
    tq = _pick_tile(512, seq)
    sub = min(256, tq)
    nq = seq // tq
    n_g = C_HEADS // C_KV
    lat_ctx_blocks = n_batch * seq // ctx_len

    def q_map(b, k, i):
        return (b * nq + i, k)

    kv_specs = [pl.BlockSpec((ctx_len, HEAD_DIM), lambda b, k, i: (lat_ctx_blocks + b, k)),
                pl.BlockSpec((seq, HEAD_DIM), lambda b, k, i: (b, k))]
    return pl.pallas_call(
        functools.partial(_attn_glob_kernel, n_g=n_g, sub=sub),
        grid=(n_batch, C_KV, nq),
        in_specs=[pl.BlockSpec((tq, n_g * HEAD_DIM), q_map)] + kv_specs + kv_specs,
        out_specs=pl.BlockSpec((tq, n_g * HEAD_DIM), q_map),
        out_shape=jax.ShapeDtypeStruct((t_rows, C_WIDTH), F32),
        compiler_params=_params(("arbitrary",) * 3),
        name="attn_global",
    )(qc, kc, kc, vc, vc)


def _attn_ctx_kernel(sink_ref, q_ref, k_ref, v_ref, o_ref, *, n_g, use_sink):
    kvh = pl.program_id(1)
    k, v = k_ref[...], v_ref[...]
    for g in range(n_g):
        sl = slice(g * HEAD_DIM, (g + 1) * HEAD_DIM)
        sink = sink_ref[kvh * n_g + g] * LOG2E if use_sink else None
        o_ref[:, sl] = _softmax_pv([_dot_nt(q_ref[:, sl], k)], [v], sink)


def _attn_ctx(sink, q, k, v, n_heads, n_kv, use_sink, dims):
    n_batch, seq, ctx_len, _ = dims
    n_g = n_heads // n_kv
    lat_ctx_blocks = n_batch * seq // ctx_len

    def blk(b, k):
        return (lat_ctx_blocks + b, k)

    return pl.pallas_call(
        functools.partial(_attn_ctx_kernel, n_g=n_g, use_sink=use_sink),
        grid=(n_batch, n_kv),
        in_specs=[pl.BlockSpec(memory_space=pltpu.SMEM),
                  pl.BlockSpec((ctx_len, n_g * HEAD_DIM), blk),
                  pl.BlockSpec((ctx_len, HEAD_DIM), blk),
                  pl.BlockSpec((ctx_len, HEAD_DIM), blk)],
        out_specs=pl.BlockSpec((ctx_len, n_g * HEAD_DIM), lambda b, k: (b, k)),
        out_shape=jax.ShapeDtypeStruct((n_batch * ctx_len, n_heads * HEAD_DIM), F32),
        compiler_params=_params(("arbitrary",) * 2),
        name="attn_context",
    )(sink, q, k, v)


def _merge_kernel(*refs, n_lat_tiles, has_ctx, split_x, sub):
    if has_ctx:
        oa_ref, oc_ref, oax_ref, ocx_ref = refs[:4]
        refs = refs[4:]
    else:
        oa_ref, oc_ref = refs[:2]
        refs = refs[2:]
    if split_x:
        x_ref, xc_ref = refs[:2]
        refs = refs[2:]
    else:
        x_ref = refs[0]
        refs = refs[1:]
    ob_ref, og_ref, w_ref, g1_ref, sh_ref, sc_ref, n2_ref, wr_ref, xo_ref, h_ref, aff_ref = refs
    is_lat = pl.program_id(0) < n_lat_tiles
    n_exp = aff_ref.shape[1]
    half = x_ref.shape[1] // 2

    def group(o, off, width):
        y = (_rms_scale(o) * og_ref[:, off:off + width]).astype(BF16)
        return jnp.dot(y, w_ref[off:off + width, :], preferred_element_type=F32)

    for r0 in range(0, x_ref.shape[0], sub):
        rows = slice(r0, r0 + sub)
        o_a, o_c = oa_ref[rows, :], oc_ref[rows, :]
        if has_ctx:
            o_a = jnp.where(is_lat, o_a, oax_ref[rows, :])
            o_c = jnp.where(is_lat, o_c, ocx_ref[rows, :])
        y = (group(o_a, 0, A_WIDTH) + group(ob_ref[rows, :], A_WIDTH, B_WIDTH)
             + group(o_c, A_WIDTH + B_WIDTH, C_WIDTH))
        x = x_ref[rows, :]
        if split_x:
            x = jnp.where(is_lat, x, xc_ref[rows, :])
        xn = x + g1_ref[0] * y
        xo_ref[rows, :] = xn
        h = _rms_scale(xn) * n2_ref[...]
        h = h * (1.0 + sc_ref[0]) + sh_ref[0]
        hb = h.astype(BF16)
        lo = lax.bitcast_convert_type(hb[:, :half].astype(F32), U32) >> 16
        hi = lax.bitcast_convert_type(hb[:, half:].astype(F32), U32) & jnp.uint32(0xFFFF0000)
        h_ref[rows, :] = lo | hi
        h_lo = (h - hb.astype(F32)).astype(BF16)
        both = jnp.dot(hb, wr_ref[...], preferred_element_type=F32)
        logits = (both[:, :n_exp] + both[:, n_exp:]
                  + jnp.dot(h_lo, wr_ref[:, :n_exp], preferred_element_type=F32))
        e = jnp.exp(logits - logits.max(axis=-1, keepdims=True))
        aff_ref[rows, :] = e / jnp.sum(e, axis=-1, keepdims=True)


def _merge(o_a, o_c, o_ax, o_cx, o_b, out_g, w_out_bf, xs, modt, norm2_g, w_router, rows_out, dims):
    n_batch, seq, ctx_len, d = dims
    tm = _pick_tile(512, seq, n_batch * ctx_len)
    n_lat_tiles = n_batch * seq // tm
    tps = seq // tm
    n_exp = w_router.shape[1]
    mix = A_WIDTH + B_WIDTH + C_WIDTH
    has_ctx = o_ax is not None
    assert has_ctx == (rows_out > n_batch * seq)
    split_x = isinstance(xs, tuple)
    assert has_ctx or not split_x
    w_hi = w_router.astype(BF16)
    w_lo = (w_router - w_hi.astype(F32)).astype(BF16)

    def mod_spec(k):
        return pl.BlockSpec((1, 1, d), lambda i: (_row_seg(i, n_lat_tiles, tps, n_batch) * N_MOD + k, 0, 0))

    def rows(width):
        return pl.BlockSpec((tm, width), lambda i: (i, 0))

    def lat_rows(width):
        return pl.BlockSpec((tm, width), lambda i: (jnp.minimum(i, n_lat_tiles - 1), 0))

    def ctx_rows(width):
        return pl.BlockSpec((tm, width), lambda i: (jnp.maximum(i - n_lat_tiles, 0), 0))

    def full(shape):
        return pl.BlockSpec(shape, lambda i: (0,) * len(shape))

    attn_specs = [lat_rows(A_WIDTH), lat_rows(C_WIDTH)]
    attn_args = [o_a, o_c]
    if has_ctx:
        attn_specs += [ctx_rows(A_WIDTH), ctx_rows(C_WIDTH)]
        attn_args += [o_ax, o_cx]
    if split_x:
        attn_specs += [lat_rows(d), ctx_rows(d)]
        attn_args += list(xs)
    else:
        attn_specs += [rows(d)]
        attn_args += [xs]
    return pl.pallas_call(
        functools.partial(_merge_kernel, n_lat_tiles=n_lat_tiles, has_ctx=has_ctx, split_x=split_x,
                          sub=min(256, tm)),
        grid=(rows_out // tm,),
        in_specs=attn_specs + [rows(B_WIDTH), full((1, mix)), full((mix, d)),
                               mod_spec(2), mod_spec(3), mod_spec(4), full((1, d)), full((d, 2 * n_exp))],
        out_specs=[rows(d), rows(d // 2), rows(n_exp)],
        out_shape=[jax.ShapeDtypeStruct((rows_out, d), F32), jax.ShapeDtypeStruct((rows_out, d // 2), U32),
                   jax.ShapeDtypeStruct((rows_out, n_exp), F32)],
        compiler_params=_params(("arbitrary",)),
        name="merge_router",
    )(*attn_args, o_b, out_g, w_out_bf, modt, modt, modt, norm2_g, jnp.concatenate([w_hi, w_lo], axis=1))


def _prefix_excl(mask_f32, tri):
    rows, n = mask_f32.shape
    cw = tri.shape[0]
    carry = jnp.zeros((rows, 1), F32)
    parts = []
    for c in range(n // cw):
        m = mask_f32[:, c * cw:(c + 1) * cw]
        parts.append(jnp.dot(m.astype(BF16), tri, preferred_element_type=F32) + carry)
        carry = carry + jnp.sum(m, axis=-1, keepdims=True)
    return parts[0] if len(parts) == 1 else jnp.concatenate(parts, axis=-1)


def _route_kernel(aff_ref, idx_ref, gate_ref, *, n, cap, n_exp, cb):
    a = aff_ref[0]
    bits = pltpu.bitcast(a, I32)

    def bit_step(b, thr):
        cand = thr | jnp.left_shift(jnp.int32(1), 30 - b)
        cnt = jnp.sum(jnp.where(bits >= cand, 1.0, 0.0), axis=-1, keepdims=True)
        return jnp.where(cnt >= cap, cand, thr)

    thr = lax.fori_loop(0, 31, bit_step, jnp.zeros((n_exp, 1), I32))
    gt = bits > thr
    eq = bits == thr
    cw = min(256, n)
    tri = jnp.where(lax.broadcasted_iota(I32, (cw, cw), 0) < lax.broadcasted_iota(I32, (cw, cw), 1),
                    1.0, 0.0).astype(BF16)
    need = cap - jnp.sum(jnp.where(gt, 1.0, 0.0), axis=-1, keepdims=True)
    sel = gt | (eq & (_prefix_excl(jnp.where(eq, 1.0, 0.0), tri) < need))
    pos = jnp.where(sel, _prefix_excl(jnp.where(sel, 1.0, 0.0), tri), -1.0)

    tok = lax.broadcasted_iota(I32, (1, n), 1)
    t_hi = (tok // 64).astype(F32)
    t_lo = (tok % 64).astype(F32)
    g_hi = a.astype(BF16).astype(F32)
    r1 = a - g_hi
    g_mid = r1.astype(BF16).astype(F32)
    g_lo = r1 - g_mid
    ca = cap // cb
    pa = jnp.floor(pos * (1.0 / cb))
    pb = pos - pa * cb
    n_val = 5
    rows_l = -(-n_val * ca // 16) * 16
    rowi = lax.broadcasted_iota(I32, (rows_l, n), 0)
    which = rowi // ca
    pa_row = (rowi % ca).astype(F32)
    pb_row = lax.broadcasted_iota(I32, (cb, n), 0).astype(F32)
    for e in range(n_exp):
        one = slice(e, e + 1)
        val = jnp.where(which == 0, t_hi,
              jnp.where(which == 1, t_lo,
              jnp.where(which == 2, g_hi[one, :],
              jnp.where(which == 3, g_mid[one, :],
              jnp.where(which == 4, g_lo[one, :], 0.0)))))
        lhs = jnp.where(pa[one, :] == pa_row, val, 0.0).astype(BF16)
        rhs = jnp.where(pb[one, :] == pb_row, 1.0, 0.0).astype(BF16)
        res = _dot_nt(lhs, rhs)
        part = lambda k: res[k * ca:(k + 1) * ca, :]
        idx_ref[0, e * ca:(e + 1) * ca, :] = (part(0) * 64.0 + part(1)).astype(I32)
        gate_ref[0, e * ca:(e + 1) * ca, :] = part(2) + part(3) + part(4)


def _route(aff_sets):
    n_sets, n_exp, n = aff_sets.shape
    cap = EC_CAPACITY * n // n_exp
    cb = min(32, cap)
    assert cap % cb == 0 and cb & (cb - 1) == 0
    spec = pl.BlockSpec((1, n_exp * cap // cb, cb), lambda s: (s, 0, 0))
    idx, gate = pl.pallas_call(
        functools.partial(_route_kernel, n=n, cap=cap, n_exp=n_exp, cb=cb),
        grid=(n_sets,),
        in_specs=[pl.BlockSpec((1, n_exp, n), lambda s: (s, 0, 0))],
        out_specs=[spec, spec],
        out_shape=[jax.ShapeDtypeStruct((n_sets, n_exp * cap // cb, cb), I32),
                   jax.ShapeDtypeStruct((n_sets, n_exp * cap // cb, cb), F32)],
        compiler_params=_params(("arbitrary",)),
        name="route",
    )(aff_sets)
    return idx.reshape(n_sets, n_exp, cap), gate.reshape(n_sets, n_exp, cap)


def _ffn_kernel(idx_ref, hpk_hbm, gate_ref, g2_ref, wg_ref, wu_ref, wd_ref, x_in_hbm, x_hbm,
                hpk, hb, yacc, xbuf, sem_h, sem_x, sem_s, *, m, n_f, n_groups, groups, n_split, per, ctx_seg):
    del x_in_hbm
    e = pl.program_id(0)
    part = pl.program_id(1)
    f = pl.program_id(2)
    g = e * n_split + part
    slot = g % 2
    g_prev = jnp.maximum(g - 1, 0)
    g_next = (g + 1) % n_groups
    half = n_f // 2
    rows_rw = m // half
    h_steps = max(n_f - 1, 1)
    h_base = -(-m // (h_steps * SUBLANES)) * SUBLANES
    h_parts = [(min(k * h_base, m), min((k + 1) * h_base, m)) for k in range(h_steps)]

    def h_copy(r, j, dst_slot):
        return pltpu.make_async_copy(hpk_hbm.at[pl.ds(r, 1), :], hpk.at[dst_slot, pl.ds(j, 1), :],
                                     sem_h.at[dst_slot])

    def x_copy(r, j):
        return pltpu.make_async_copy(x_hbm.at[pl.ds(r, 1), :], xbuf.at[pl.ds(j, 1), :], sem_x)

    def s_copy(r, j):
        return pltpu.make_async_copy(xbuf.at[pl.ds(j, 1), :], x_hbm.at[pl.ds(r, 1), :], sem_s)

    def start_rows(copy, grp, first, n):
        for j in range(n):
            copy(idx_ref[grp * m + first + j], first + j).start()

    def wait_h(s):
        pltpu.make_async_copy(hpk.at[1 - s], hpk.at[s], sem_h.at[s]).wait()

    def wait_rows(sem):
        pltpu.make_async_copy(yacc, xbuf, sem).wait()

    def for_rows(fn):
        def body(j, carry):
            fn(j)
            return carry
        lax.fori_loop(0, m, body, 0)

    @pl.when((g == 0) & (f == 0))
    def _():
        for_rows(lambda j: h_copy(idx_ref[j], j, 0).start())
        for_rows(lambda j: x_copy(idx_ref[j], j).start())
        wait_rows(sem_x)

    @pl.when(f == 0)
    def _():
        wait_h(slot)
        p = hpk[slot]
        d_half = hpk.shape[2]
        hb[:, :d_half] = lax.bitcast_convert_type(p << 16, F32).astype(BF16)
        hb[:, d_half:] = lax.bitcast_convert_type(p & jnp.uint32(0xFFFF0000), F32).astype(BF16)

    @pl.when(f == half)
    def _():
        wait_rows(sem_s)

    def matmuls(first):
        h = hb[...]
        a = jnp.dot(h, wg_ref[...].astype(BF16), preferred_element_type=F32)
        u = jnp.dot(h, wu_ref[...].astype(BF16), preferred_element_type=F32)
        z = (a / (1.0 + jnp.exp(-a)) * u).astype(BF16)
        y = jnp.dot(z, wd_ref[...].astype(BF16), preferred_element_type=F32)
        if first:
            yacc[...] = y
        else:
            yacc[...] += y

    for step in range(n_f):
        @pl.when(f == step)
        def _(step=step):
            if step < half:
                start_rows(s_copy, g_prev, step * rows_rw, rows_rw)
            else:
                start_rows(x_copy, g, (step - half) * rows_rw, rows_rw)
            if step < h_steps:
                start_rows(lambda r, j: h_copy(r, j, 1 - slot), g_next, h_parts[step][0],
                           h_parts[step][1] - h_parts[step][0])
            matmuls(first=step == 0)

    @pl.when(f == n_f - 1)
    def _():
        wait_rows(sem_x)
        for off, size, sample in groups:
            seg = ctx_seg if sample is None else part * per + sample
            rows = pl.ds(off, size)
            xbuf[rows, :] = xbuf[rows, :] + g2_ref[pl.ds(seg, 1), :] * (gate_ref[0, 0, rows, :] * yacc[rows, :])

    @pl.when((g == n_groups - 1) & (f == n_f - 1))
    def _():
        for_rows(lambda j: s_copy(idx_ref[g * m + j], j).start())
        wait_rows(sem_s)
        wait_h(1 - slot)


def _experts(idx_flat, gate, hpk, g2tab, w_gate, w_up, w_down, xs, layer, m, groups, n_split, per, ctx_seg):
    d = xs.shape[1]
    n_exp, d_exp = w_gate.shape[1], w_gate.shape[3]
    tf = _pick_tile(256, d_exp // 2)
    n_f = d_exp // tf
    n_groups = n_exp * n_split
    assert n_f % 2 == 0 and m % (n_f // 2) == 0 and n_groups % 2 == 0
    grid_spec = pltpu.PrefetchScalarGridSpec(
        num_scalar_prefetch=1,
        grid=(n_exp, n_split, n_f),
        in_specs=[
            pl.BlockSpec(memory_space=pl.ANY),
            pl.BlockSpec((1, 1, m, 1), lambda e, p, f, idx: (e, p, 0, 0)),
            pl.BlockSpec(g2tab.shape, lambda e, p, f, idx: (0, 0)),
            pl.BlockSpec((None, None, d, tf), lambda e, p, f, idx: (layer, e, 0, f)),
            pl.BlockSpec((None, None, d, tf), lambda e, p, f, idx: (layer, e, 0, f)),
            pl.BlockSpec((None, None, tf, d), lambda e, p, f, idx: (layer, e, f, 0)),
            pl.BlockSpec(memory_space=pl.ANY),
        ],
        out_specs=pl.BlockSpec(memory_space=pl.ANY),
        scratch_shapes=[pltpu.VMEM((2, m, d // 2), U32), pltpu.VMEM((m, d), BF16), pltpu.VMEM((m, d), F32),
                        pltpu.VMEM((m, d), F32), pltpu.SemaphoreType.DMA((2,)), pltpu.SemaphoreType.DMA,
                        pltpu.SemaphoreType.DMA],
    )
    return pl.pallas_call(
        functools.partial(_ffn_kernel, m=m, n_f=n_f, n_groups=n_groups, groups=tuple(groups), n_split=n_split,
                          per=per, ctx_seg=ctx_seg),
        grid_spec=grid_spec,
        out_shape=jax.ShapeDtypeStruct(xs.shape, xs.dtype),
        input_output_aliases={7: 0},
        compiler_params=_params(("arbitrary",) * 3),
        name="experts",
    )(idx_flat, hpk, gate, g2tab, w_gate, w_up, w_down, xs)


def _rope_tables(seq):
    pos = np.arange(seq)
    n_freq = HEAD_DIM // 4
    inv = jnp.asarray(ROPE_THETA, F32) ** (-jnp.arange(n_freq, dtype=F32) / n_freq)
    row = jnp.asarray(pos // GRID_W, F32)[:, None] * inv
    col = jnp.asarray(pos % GRID_W, F32)[:, None] * inv
    zero = jnp.zeros_like(row)
    cos = jnp.concatenate([jnp.cos(row), jnp.cos(row), jnp.cos(col), jnp.cos(col)], axis=-1)
    s1 = jnp.concatenate([zero, jnp.sin(row), zero, jnp.sin(col)], axis=-1)
    s2 = jnp.concatenate([-jnp.sin(row), zero, -jnp.sin(col), zero], axis=-1)

    def padded(tm):
        ident = jnp.ones((tm, HEAD_DIM), F32)
        none = jnp.zeros((tm, HEAD_DIM), F32)
        return (jnp.concatenate([cos, ident], axis=0), jnp.concatenate([s1, none], axis=0),
                jnp.concatenate([s2, none], axis=0))

    return padded


def _expert_rows(idx_l, gate_l, idx_c, gate_c, dims, n_split):
    n_batch, seq, ctx_len, _ = dims
    per = n_batch // n_split
    n_exp = idx_l.shape[1]

    def arrange(v):
        s = v.shape[-1]
        return v.reshape(n_split, per, n_exp, s).transpose(2, 0, 1, 3).reshape(n_exp, n_split, per * s)

    rows_l = idx_l + (jnp.arange(n_batch, dtype=I32) * seq)[:, None, None]
    parts_i, parts_g = [arrange(rows_l)], [arrange(gate_l)]
    groups = [(b * idx_l.shape[-1], idx_l.shape[-1], b) for b in range(per)]
    if idx_c is not None:
        rows_c = idx_c + (n_batch * seq + jnp.arange(n_batch, dtype=I32) * ctx_len)[:, None, None]
        parts_i.append(arrange(rows_c))
        parts_g.append(arrange(gate_c))
        groups += [(per * idx_l.shape[-1], per * idx_c.shape[-1], None)]
    idx = jnp.concatenate(parts_i, axis=-1)
    gate = jnp.concatenate(parts_g, axis=-1)
    return idx, gate, groups


def kernel(x, c, ctx, c_ctx, w_mod, b_mod, norm1_g, norm2_g, w_in, qn_a, kn_a, sink_a, vn_b, w_s, b_s,
           qn_c, kn_c, out_g, w_out, w_router, w_gate, w_up, w_down):
    n_batch, seq, d = x.shape
    ctx_len = ctx.shape[1]
    depth = w_mod.shape[0]
    n_exp = w_router.shape[2]
    dims = (n_batch, seq, ctx_len, d)
    t_lat = n_batch * seq
    assert n_batch < MOD_ROWS and seq % WINDOW == 0 and ctx_len % B_CHUNK == 0 and t_lat % ctx_len == 0
    n_split = 2 if n_batch % 2 == 0 else 1
    per = n_batch // n_split

    xs = (x.reshape(t_lat, d), ctx.reshape(n_batch * ctx_len, d))
    cvec = jnp.zeros((MOD_ROWS, d), F32).at[:n_batch].set(c).at[n_batch].set(c_ctx)
    rope = _rope_tables(seq)

    for layer in range(depth):
        last = layer == depth - 1
        mod = _modulation(cvec, w_mod, b_mod, layer)
        modt = mod.reshape(MOD_ROWS * N_MOD, 1, d)
        g2tab = mod.reshape(MOD_ROWS, N_MOD, d)[:, N_MOD - 1, :]
        row = lambda v: v[layer].reshape(1, -1)
        bsb = jnp.broadcast_to(b_s[layer][:, :, None], (B_GROUPS, B_CHUNK, HEAD_DIM))
        qa, ka, va, qc, kc, vc, o_b = _inproj(
            xs, modt, row(norm1_g), w_in[layer].astype(BF16), rope, row(qn_a), row(kn_a), row(qn_c),
            row(kn_c), row(vn_b), w_s[layer].astype(BF16), bsb, dims)
        o_a = _attn_win(sink_a[layer], qa, ka, va, dims)
        o_c = _attn_glob(qc, kc, vc, dims)
        o_ax = o_cx = None
        if not last:
            o_ax = _attn_ctx(sink_a[layer], qa, ka, va, A_HEADS, A_KV, True, dims)
            o_cx = _attn_ctx(sink_a[layer], qc, kc, vc, C_HEADS, C_KV, False, dims)
        rows_out = t_lat if last else t_lat + n_batch * ctx_len
        xs, h2, aff = _merge(o_a, o_c, o_ax, o_cx, o_b, row(out_g), w_out[layer].astype(BF16), xs, modt,
                             row(norm2_g), w_router[layer], rows_out, dims)

        idx_l, gate_l = _route(aff[:t_lat].reshape(n_batch, seq, n_exp).transpose(0, 2, 1))
        idx_c = gate_c = None
        if not last:
            idx_c, gate_c = _route(aff[t_lat:].reshape(n_batch, ctx_len, n_exp).transpose(0, 2, 1))
        idx, gate, groups = _expert_rows(idx_l, gate_l, idx_c, gate_c, dims, n_split)
        m = idx.shape[-1]
        xs = _experts(idx.reshape(-1), gate.reshape(n_exp, n_split, m, 1), h2, g2tab, w_gate, w_up, w_down,
                      xs, layer, m, groups, n_split, per, n_batch)
    return xs.reshape(n_batch, seq, d)
```

```python
import functools

import jax
import jax.numpy as jnp
import numpy as np
from jax import lax
from jax.experimental import pallas as pl
from jax.experimental.pallas import tpu as pltpu

F32 = jnp.float32
BF16 = jnp.bfloat16
I32 = jnp.int32
U32 = jnp.uint32

HEAD_DIM = 128
GRID_W = 64
WINDOW = 128
A_HEADS, A_KV = 6, 2
B_GROUPS, B_CHUNK = 4, 128
C_HEADS, C_KV = 6, 2
A_WIDTH = A_HEADS * HEAD_DIM
B_WIDTH = B_GROUPS * HEAD_DIM
C_WIDTH = C_HEADS * HEAD_DIM
KV_WIDTH = A_KV * HEAD_DIM
OFF_QA = 0
OFF_KA = OFF_QA + A_WIDTH
OFF_VA = OFF_KA + KV_WIDTH
OFF_UB = OFF_VA + KV_WIDTH
OFF_VB = OFF_UB + B_WIDTH
OFF_QC = OFF_VB + B_WIDTH
OFF_KC = OFF_QC + C_WIDTH
OFF_VC = OFF_KC + KV_WIDTH
IN_WIDTH = OFF_VC + KV_WIDTH
EC_CAPACITY = 2
N_MOD = 6
ROPE_THETA = 10000.0
EPS = 1e-6
NEG_INF = -1e30
LOG2E = 1.4426950408889634
QK_SCALE = HEAD_DIM ** -0.5 * LOG2E
SUBLANES = 8
PIPELINE_DEPTH = 2
MOD_ROWS = 8

VMEM_LIMIT = 56 * 1024 * 1024

def _params(sem, vmem=VMEM_LIMIT):
    return pltpu.CompilerParams(dimension_semantics=sem, vmem_limit_bytes=vmem)


def _pick_tile(target, *sizes):
    t = target
    while any(s % t for s in sizes):
        t //= 2
    return t


def _rms_scale(x):
    return x * lax.rsqrt(jnp.mean(x * x, axis=-1, keepdims=True) + EPS)


def _dot_nt(a, b):
    return lax.dot_general(a, b, (((1,), (1,)), ((), ())), preferred_element_type=F32)


def _mod_kernel(c_ref, w_ref, b_ref, o_ref):
    c = c_ref[...]
    s = c / (1.0 + jnp.exp(-c))
    o_ref[...] = jnp.dot(s, w_ref[...], precision=lax.Precision.HIGHEST,
                         preferred_element_type=F32) + b_ref[...]


def _modulation(cvec, w_mod, b_mod, layer):
    d = cvec.shape[1]
    n = w_mod.shape[2]
    tn = _pick_tile(1024, n)
    return pl.pallas_call(
        _mod_kernel,
        grid=(n // tn,),
        in_specs=[
            pl.BlockSpec((MOD_ROWS, d), lambda j: (0, 0)),
            pl.BlockSpec((None, d, tn), lambda j: (layer, 0, j)),
            pl.BlockSpec((None, 1, tn), lambda j: (layer, 0, j)),
        ],
        out_specs=pl.BlockSpec((MOD_ROWS, tn), lambda j: (0, j)),
        out_shape=jax.ShapeDtypeStruct((MOD_ROWS, n), F32),
        compiler_params=_params(("arbitrary",)),
        name="modulation",
    )(cvec, w_mod, b_mod.reshape(b_mod.shape[0], 1, n))


def _inproj_kernel(*refs, tm, n_lat_tiles, split_input, sub):
    if split_input:
        xl_ref, xc_ref = refs[:2]
        refs = refs[2:]
    else:
        xl_ref = refs[0]
        refs = refs[1:]
    (sh_ref, sc_ref, g_ref, w_ref, cos_ref, s1_ref, s2_ref, qna_ref, kna_ref, qnc_ref, knc_ref, vnb_ref, ws_ref,
     bsb_ref, qa_ref, ka_ref, va_ref, qc_ref, kc_ref, vc_ref, ob_ref) = refs

    for r0 in range(0, tm, sub):
        rows = slice(r0, r0 + sub)
        x = xl_ref[rows, :]
        if split_input:
            x = jnp.where(pl.program_id(0) < n_lat_tiles, x, xc_ref[rows, :])
        h = _rms_scale(x) * g_ref[...]
        h = h * (1.0 + sc_ref[0]) + sh_ref[0]
        hb = h.astype(BF16)
        cos, s1, s2 = cos_ref[rows, :], s1_ref[rows, :], s2_ref[rows, :]

        def proj(off, width):
            return jnp.dot(hb, w_ref[:, off:off + width], preferred_element_type=F32)

        def qk_head(p, gain, scale):
            q = _rms_scale(p) * gain
            q = q * cos + pltpu.roll(q, 32, 1) * s1 + pltpu.roll(q, 96, 1) * s2
            if scale != 1.0:
                q = q * scale
            return q.astype(BF16)

        def heads(off, n_heads, gain_ref, scale, out_ref):
            p = proj(off, n_heads * HEAD_DIM)
            for hd in range(n_heads):
                sl = slice(hd * HEAD_DIM, (hd + 1) * HEAD_DIM)
                out_ref[rows, sl] = qk_head(p[:, sl], gain_ref[...], scale)

        pu = proj(OFF_UB, B_WIDTH)
        pv = proj(OFF_VB, B_WIDTH)
        heads(OFF_QA, A_HEADS, qna_ref, QK_SCALE, qa_ref)
        heads(OFF_KA, A_KV, kna_ref, 1.0, ka_ref)
        va_ref[rows, :] = proj(OFF_VA, KV_WIDTH).astype(BF16)
        heads(OFF_QC, C_HEADS, qnc_ref, QK_SCALE, qc_ref)
        heads(OFF_KC, C_KV, knc_ref, 1.0, kc_ref)
        vc_ref[rows, :] = proj(OFF_VC, KV_WIDTH).astype(BF16)

        for g in range(B_GROUPS):
            sl = slice(g * HEAD_DIM, (g + 1) * HEAD_DIM)
            u = jax.nn.gelu(pu[:, sl])
            vn = (_rms_scale(jax.nn.gelu(pv[:, sl])) * vnb_ref[:, sl]).astype(BF16)
            for c in range(sub // B_CHUNK):
                chunk = slice(c * B_CHUNK, (c + 1) * B_CHUNK)
                mixed = jnp.dot(ws_ref[g], vn[chunk, :], preferred_element_type=F32) + bsb_ref[g]
                ob_ref[r0 + c * B_CHUNK:r0 + (c + 1) * B_CHUNK, sl] = u[chunk, :] * mixed


def _row_seg(i, n_lat_tiles, tiles_per_seq, n_batch):
    return jnp.where(i < n_lat_tiles, i // tiles_per_seq, n_batch)


def _inproj(xs, modt, norm_g, w_in_bf, rope, qn_a, kn_a, qn_c, kn_c, vn_b, ws_bf, bsb, dims):
    n_batch, seq, ctx_len, d = dims
    split_input = isinstance(xs, tuple)
    t_rows = n_batch * (seq + ctx_len)
    tm = _pick_tile(512, seq, n_batch * ctx_len)
    n_lat_tiles = n_batch * seq // tm
    tps = seq // tm
    cos_t, s1_t, s2_t = rope(tm)
    if split_input:
        x_specs = [pl.BlockSpec((tm, d), lambda i: (jnp.minimum(i, n_lat_tiles - 1), 0)),
                   pl.BlockSpec((tm, d), lambda i: (jnp.maximum(i - n_lat_tiles, 0), 0))]
        x_args = list(xs)
    else:
        x_specs = [pl.BlockSpec((tm, d), lambda i: (i, 0))]
        x_args = [xs]

    def mod_spec(k):
        return pl.BlockSpec((1, 1, d), lambda i: (_row_seg(i, n_lat_tiles, tps, n_batch) * N_MOD + k, 0, 0))

    def rope_spec():
        return pl.BlockSpec((tm, HEAD_DIM), lambda i: (jnp.where(i < n_lat_tiles, i % tps, tps), 0))

    def full(shape):
        return pl.BlockSpec(shape, lambda i: (0,) * len(shape))

    def out(width, dtype):
        return pl.BlockSpec((tm, width), lambda i: (i, 0)), jax.ShapeDtypeStruct((t_rows, width), dtype)

    outs = [out(A_WIDTH, BF16), out(KV_WIDTH, BF16), out(KV_WIDTH, BF16),
            out(C_WIDTH, BF16), out(KV_WIDTH, BF16), out(KV_WIDTH, BF16), out(B_WIDTH, F32)]
    return pl.pallas_call(
        functools.partial(_inproj_kernel, tm=tm, n_lat_tiles=n_lat_tiles, split_input=split_input,
                          sub=min(256, tm)),
        grid=(t_rows // tm,),
        in_specs=x_specs + [
            mod_spec(0), mod_spec(1),
            full((1, d)),
            full((d, IN_WIDTH)),
            rope_spec(), rope_spec(), rope_spec(),
            full((1, HEAD_DIM)), full((1, HEAD_DIM)), full((1, HEAD_DIM)), full((1, HEAD_DIM)),
            full((1, B_WIDTH)),
            full((B_GROUPS, B_CHUNK, B_CHUNK)),
            full((B_GROUPS, B_CHUNK, HEAD_DIM)),
        ],
        out_specs=[o[0] for o in outs],
        out_shape=[o[1] for o in outs],
        compiler_params=_params(("arbitrary",)),
        name="inproj",
    )(*x_args, modt, modt, norm_g, w_in_bf, cos_t, s1_t, s2_t, qn_a, kn_a, qn_c, kn_c, vn_b, ws_bf, bsb)


def _softmax_pv(s_list, v_list, sink):
    m = s_list[0].max(axis=-1, keepdims=True)
    for s in s_list[1:]:
        m = jnp.maximum(m, s.max(axis=-1, keepdims=True))
    if sink is not None:
        m = jnp.maximum(m, sink)
    l = None
    o = None
    for s, v in zip(s_list, v_list):
        p = jnp.exp2(s - m)
        ls = jnp.sum(p, axis=-1, keepdims=True)
        os_ = jnp.dot(p.astype(BF16), v, preferred_element_type=F32)
        l = ls if l is None else l + ls
        o = os_ if o is None else o + os_
    if sink is not None:
        l = l + jnp.exp2(sink - m)
    return o / l


def _attn_win_kernel(sink_ref, q_ref, kx_ref, kp_ref, km_ref, kn_ref, vx_ref, vp_ref, vm_ref, vn_ref,
                     o_ref, *, tq, seq, n_g):
    i = pl.program_id(1)
    kvh = pl.program_id(2)
    kband = jnp.concatenate([kp_ref[...], km_ref[...], kn_ref[...]], axis=0)
    vband = jnp.concatenate([vp_ref[...], vm_ref[...], vn_ref[...]], axis=0)
    kx, vx = kx_ref[...], vx_ref[...]
    nb = 3 * WINDOW
    rows_all = n_g * WINDOW
    r = lax.broadcasted_iota(I32, (rows_all, nb), 0) % WINDOW
    c = lax.broadcasted_iota(I32, (rows_all, nb), 1)
    head = lax.broadcasted_iota(I32, (rows_all, 1), 0) // WINDOW
    sink = sink_ref[kvh * n_g] * LOG2E
    for g in range(1, n_g):
        sink = jnp.where(head == g, sink_ref[kvh * n_g + g] * LOG2E, sink)

    def scores(j):
        rows = slice(j * WINDOW, (j + 1) * WINDOW)
        q = jnp.concatenate([q_ref[rows, g * HEAD_DIM:(g + 1) * HEAD_DIM] for g in range(n_g)], axis=0)
        kpos = c + (i * tq + (j - 1) * WINDOW)
        valid = (jnp.abs(c - WINDOW - r) <= WINDOW) & (kpos >= 0) & (kpos < seq)
        s_band = jnp.where(valid, _dot_nt(q, kband[j * WINDOW:j * WINDOW + nb, :]), NEG_INF)
        return [_dot_nt(q, kx), s_band]

    n_blocks = tq // WINDOW
    pending = [scores(j) for j in range(min(PIPELINE_DEPTH, n_blocks))]
    for j in range(n_blocks):
        s_cur = pending.pop(0)
        if j + PIPELINE_DEPTH < n_blocks:
            pending.append(scores(j + PIPELINE_DEPTH))
        o = _softmax_pv(s_cur, [vx, vband[j * WINDOW:j * WINDOW + nb, :]], sink)
        for g in range(n_g):
            o_ref[j * WINDOW:(j + 1) * WINDOW, g * HEAD_DIM:(g + 1) * HEAD_DIM] = o[g * WINDOW:(g + 1) * WINDOW, :]


def _attn_win(sink, qa, ka, va, dims):
    n_batch, seq, ctx_len, _ = dims
    t_rows = n_batch * seq
    tq = _pick_tile(1024, seq)
    nq = seq // tq
    n_g = A_HEADS // A_KV
    wb = tq // WINDOW
    sb = seq // WINDOW
    lat_ctx_blocks = n_batch * seq // ctx_len

    def q_map(b, i, k):
        return (b * nq + i, k)

    def prev_map(b, i, k):
        return (b * sb + jnp.maximum(i * wb - 1, 0), k)

    def next_map(b, i, k):
        return (b * sb + jnp.minimum((i + 1) * wb, sb - 1), k)

    def ctx_map(b, i, k):
        return (lat_ctx_blocks + b, k)

    kv_specs = [pl.BlockSpec((ctx_len, HEAD_DIM), ctx_map), pl.BlockSpec((WINDOW, HEAD_DIM), prev_map),
                pl.BlockSpec((tq, HEAD_DIM), q_map), pl.BlockSpec((WINDOW, HEAD_DIM), next_map)]
    return pl.pallas_call(
        functools.partial(_attn_win_kernel, tq=tq, seq=seq, n_g=n_g),
        grid=(n_batch, nq, A_KV),
        in_specs=[pl.BlockSpec(memory_space=pltpu.SMEM),
                  pl.BlockSpec((tq, n_g * HEAD_DIM), q_map)] + kv_specs + kv_specs,
        out_specs=pl.BlockSpec((tq, n_g * HEAD_DIM), q_map),
        out_shape=jax.ShapeDtypeStruct((t_rows, A_WIDTH), F32),
        compiler_params=_params(("arbitrary",) * 3),
        name="attn_window",
    )(sink, qa, ka, ka, ka, ka, va, va, va, va)


def _attn_glob_kernel(q_ref, kx_ref, kl_ref, vx_ref, vl_ref, o_ref, s_ref, *, n_g, sub):
    kx, kl, vx, vl = kx_ref[...], kl_ref[...], vx_ref[...], vl_ref[...]
    n_ctx = kx.shape[0]
    chains = [(slice(r0, r0 + sub), slice(g * HEAD_DIM, (g + 1) * HEAD_DIM))
              for r0 in range(0, q_ref.shape[0], sub) for g in range(n_g)]
    n_slots = s_ref.shape[0]

    def scores(i):
        q = q_ref[chains[i][0], chains[i][1]]
        s_ref[i % n_slots, :, :n_ctx] = _dot_nt(q, kx)
        s_ref[i % n_slots, :, n_ctx:] = _dot_nt(q, kl)

    for i in range(min(PIPELINE_DEPTH, len(chains))):
        scores(i)
    for i, chain in enumerate(chains):
        if i + PIPELINE_DEPTH < len(chains):
            scores(i + PIPELINE_DEPTH)
        slot = i % n_slots
        m = s_ref[slot].max(axis=-1, keepdims=True)
        p_x = jnp.exp2(s_ref[slot, :, :n_ctx] - m)
        p_l = jnp.exp2(s_ref[slot, :, n_ctx:] - m)
        l = jnp.sum(p_x, axis=-1, keepdims=True) + jnp.sum(p_l, axis=-1, keepdims=True)
        o = (jnp.dot(p_x.astype(BF16), vx, preferred_element_type=F32)
             + jnp.dot(p_l.astype(BF16), vl, preferred_element_type=F32))
        o_ref[chain[0], chain[1]] = o / l


def _attn_glob(qc, kc, vc, dims):
    n_batch, seq, ctx_len, _ = dims
    t_rows = n_batch * seq
    tq = _pick_tile(512, seq)
    sub = min(256, tq)
    nq = seq // tq
    n_g = C_HEADS // C_KV
    lat_ctx_blocks = n_batch * seq // ctx_len

    def q_map(b, k, i):
        return (b * nq + i, k)

    kv_specs = [pl.BlockSpec((ctx_len, HEAD_DIM), lambda b, k, i: (lat_ctx_blocks + b, k)),
                pl.BlockSpec((seq, HEAD_DIM), lambda b, k, i: (b, k))]
    return pl.pallas_call(
        functools.partial(_attn_glob_kernel, n_g=n_g, sub=sub),
        grid=(n_batch, C_KV, nq),
        in_specs=[pl.BlockSpec((tq, n_g * HEAD_DIM), q_map)] + kv_specs + kv_specs,
        out_specs=pl.BlockSpec((tq, n_g * HEAD_DIM), q_map),
        out_shape=jax.ShapeDtypeStruct((t_rows, C_WIDTH), F32),
        scratch_shapes=[pltpu.VMEM((PIPELINE_DEPTH + 1, sub, ctx_len + seq), F32)],
        compiler_params=_params(("arbitrary",) * 3),
        name="attn_global",
    )(qc, kc, kc, vc, vc)


def _attn_ctx_kernel(sink_ref, q_ref, k_ref, v_ref, o_ref, *, n_g, use_sink):
    kvh = pl.program_id(1)
    k, v = k_ref[...], v_ref[...]
    for g in range(n_g):
        sl = slice(g * HEAD_DIM, (g + 1) * HEAD_DIM)
        sink = sink_ref[kvh * n_g + g] * LOG2E if use_sink else None
        o_ref[:, sl] = _softmax_pv([_dot_nt(q_ref[:, sl], k)], [v], sink)


def _attn_ctx(sink, q, k, v, n_heads, n_kv, use_sink, dims):
    n_batch, seq, ctx_len, _ = dims
    n_g = n_heads // n_kv
    lat_ctx_blocks = n_batch * seq // ctx_len

    def blk(b, k):
        return (lat_ctx_blocks + b, k)

    return pl.pallas_call(
        functools.partial(_attn_ctx_kernel, n_g=n_g, use_sink=use_sink),
        grid=(n_batch, n_kv),
        in_specs=[pl.BlockSpec(memory_space=pltpu.SMEM),
                  pl.BlockSpec((ctx_len, n_g * HEAD_DIM), blk),
                  pl.BlockSpec((ctx_len, HEAD_DIM), blk),
                  pl.BlockSpec((ctx_len, HEAD_DIM), blk)],
        out_specs=pl.BlockSpec((ctx_len, n_g * HEAD_DIM), lambda b, k: (b, k)),
        out_shape=jax.ShapeDtypeStruct((n_batch * ctx_len, n_heads * HEAD_DIM), F32),
        compiler_params=_params(("arbitrary",) * 2),
        name="attn_context",
    )(sink, q, k, v)


def _merge_kernel(*refs, n_lat_tiles, has_ctx, split_x, sub):
    if has_ctx:
        oa_ref, oc_ref, oax_ref, ocx_ref = refs[:4]
        refs = refs[4:]
    else:
        oa_ref, oc_ref = refs[:2]
        refs = refs[2:]
    if split_x:
        x_ref, xc_ref = refs[:2]
        refs = refs[2:]
    else:
        x_ref = refs[0]
        refs = refs[1:]
    ob_ref, og_ref, w_ref, g1_ref, sh_ref, sc_ref, n2_ref, wr_ref, xo_ref, h_ref, aff_ref = refs
    is_lat = pl.program_id(0) < n_lat_tiles
    n_exp = aff_ref.shape[1]
    half = x_ref.shape[1] // 2

    def group(o, off, width):
        y = (_rms_scale(o) * og_ref[:, off:off + width]).astype(BF16)
        return jnp.dot(y, w_ref[off:off + width, :], preferred_element_type=F32)

    for r0 in range(0, x_ref.shape[0], sub):
        rows = slice(r0, r0 + sub)
        o_a, o_c = oa_ref[rows, :], oc_ref[rows, :]
        if has_ctx:
            o_a = jnp.where(is_lat, o_a, oax_ref[rows, :])
            o_c = jnp.where(is_lat, o_c, ocx_ref[rows, :])
        y = (group(o_a, 0, A_WIDTH) + group(ob_ref[rows, :], A_WIDTH, B_WIDTH)
             + group(o_c, A_WIDTH + B_WIDTH, C_WIDTH))
        x = x_ref[rows, :]
        if split_x:
            x = jnp.where(is_lat, x, xc_ref[rows, :])
        xn = x + g1_ref[0] * y
        xo_ref[rows, :] = xn
        h = _rms_scale(xn) * n2_ref[...]
        h = h * (1.0 + sc_ref[0]) + sh_ref[0]
        hb = h.astype(BF16)
        lo = lax.bitcast_convert_type(hb[:, :half].astype(F32), U32) >> 16
        hi = lax.bitcast_convert_type(hb[:, half:].astype(F32), U32) & jnp.uint32(0xFFFF0000)
        h_ref[rows, :] = lo | hi
        h_lo = (h - hb.astype(F32)).astype(BF16)
        both = jnp.dot(hb, wr_ref[...], preferred_element_type=F32)
        logits = (both[:, :n_exp] + both[:, n_exp:]
                  + jnp.dot(h_lo, wr_ref[:, :n_exp], preferred_element_type=F32))
        e = jnp.exp(logits - logits.max(axis=-1, keepdims=True))
        aff_ref[rows, :] = e / jnp.sum(e, axis=-1, keepdims=True)


def _merge(o_a, o_c, o_ax, o_cx, o_b, out_g, w_out_bf, xs, modt, norm2_g, w_router, rows_out, dims):
    n_batch, seq, ctx_len, d = dims
    tm = _pick_tile(512, seq, n_batch * ctx_len)
    n_lat_tiles = n_batch * seq // tm
    tps = seq // tm
    n_exp = w_router.shape[1]
    mix = A_WIDTH + B_WIDTH + C_WIDTH
    has_ctx = o_ax is not None
    assert has_ctx == (rows_out > n_batch * seq)
    split_x = isinstance(xs, tuple)
    assert has_ctx or not split_x
    w_hi = w_router.astype(BF16)
    w_lo = (w_router - w_hi.astype(F32)).astype(BF16)

    def mod_spec(k):
        return pl.BlockSpec((1, 1, d), lambda i: (_row_seg(i, n_lat_tiles, tps, n_batch) * N_MOD + k, 0, 0))

    def rows(width):
        return pl.BlockSpec((tm, width), lambda i: (i, 0))

    def lat_rows(width):
        return pl.BlockSpec((tm, width), lambda i: (jnp.minimum(i, n_lat_tiles - 1), 0))

    def ctx_rows(width):
        return pl.BlockSpec((tm, width), lambda i: (jnp.maximum(i - n_lat_tiles, 0), 0))

    def full(shape):
        return pl.BlockSpec(shape, lambda i: (0,) * len(shape))

    attn_specs = [lat_rows(A_WIDTH), lat_rows(C_WIDTH)]
    attn_args = [o_a, o_c]
    if has_ctx:
        attn_specs += [ctx_rows(A_WIDTH), ctx_rows(C_WIDTH)]
        attn_args += [o_ax, o_cx]
    if split_x:
        attn_specs += [lat_rows(d), ctx_rows(d)]
        attn_args += list(xs)
    else:
        attn_specs += [rows(d)]
        attn_args += [xs]
    return pl.pallas_call(
        functools.partial(_merge_kernel, n_lat_tiles=n_lat_tiles, has_ctx=has_ctx, split_x=split_x,
                          sub=min(256, tm)),
        grid=(rows_out // tm,),
        in_specs=attn_specs + [rows(B_WIDTH), full((1, mix)), full((mix, d)),
                               mod_spec(2), mod_spec(3), mod_spec(4), full((1, d)), full((d, 2 * n_exp))],
        out_specs=[rows(d), rows(d // 2), rows(n_exp)],
        out_shape=[jax.ShapeDtypeStruct((rows_out, d), F32), jax.ShapeDtypeStruct((rows_out, d // 2), U32),
                   jax.ShapeDtypeStruct((rows_out, n_exp), F32)],
        compiler_params=_params(("arbitrary",)),
        name="merge_router",
    )(*attn_args, o_b, out_g, w_out_bf, modt, modt, modt, norm2_g, jnp.concatenate([w_hi, w_lo], axis=1))


def _prefix_excl(mask_f32, tri):
    rows, n = mask_f32.shape
    cw = tri.shape[0]
    carry = jnp.zeros((rows, 1), F32)
    parts = []
    for c in range(n // cw):
        m = mask_f32[:, c * cw:(c + 1) * cw]
        parts.append(jnp.dot(m.astype(BF16), tri, preferred_element_type=F32) + carry)
        carry = carry + jnp.sum(m, axis=-1, keepdims=True)
    return parts[0] if len(parts) == 1 else jnp.concatenate(parts, axis=-1)


def _route_kernel(aff_ref, idx_ref, gate_ref, *, n, cap, n_exp, cb):
    a = aff_ref[0]
    bits = pltpu.bitcast(a, I32)

    def bit_step(b, thr):
        cand = thr | jnp.left_shift(jnp.int32(1), 30 - b)
        cnt = jnp.sum(jnp.where(bits >= cand, 1.0, 0.0), axis=-1, keepdims=True)
        return jnp.where(cnt >= cap, cand, thr)

    thr = lax.fori_loop(0, 31, bit_step, jnp.zeros((n_exp, 1), I32))
    gt = bits > thr
    eq = bits == thr
    cw = min(256, n)
    tri = jnp.where(lax.broadcasted_iota(I32, (cw, cw), 0) < lax.broadcasted_iota(I32, (cw, cw), 1),
                    1.0, 0.0).astype(BF16)
    need = cap - jnp.sum(jnp.where(gt, 1.0, 0.0), axis=-1, keepdims=True)
    sel = gt | (eq & (_prefix_excl(jnp.where(eq, 1.0, 0.0), tri) < need))
    pos = jnp.where(sel, _prefix_excl(jnp.where(sel, 1.0, 0.0), tri), -1.0)

    tok = lax.broadcasted_iota(I32, (1, n), 1)
    t_hi = (tok // 64).astype(F32)
    t_lo = (tok % 64).astype(F32)
    g_hi = a.astype(BF16).astype(F32)
    r1 = a - g_hi
    g_mid = r1.astype(BF16).astype(F32)
    g_lo = r1 - g_mid
    ca = cap // cb
    pa = jnp.floor(pos * (1.0 / cb))
    pb = pos - pa * cb
    n_val = 5
    rows_l = -(-n_val * ca // 16) * 16
    rowi = lax.broadcasted_iota(I32, (rows_l, n), 0)
    which = rowi // ca
    pa_row = (rowi % ca).astype(F32)
    pb_row = lax.broadcasted_iota(I32, (cb, n), 0).astype(F32)
    for e in range(n_exp):
        one = slice(e, e + 1)
        val = jnp.where(which == 0, t_hi,
              jnp.where(which == 1, t_lo,
              jnp.where(which == 2, g_hi[one, :],
              jnp.where(which == 3, g_mid[one, :],
              jnp.where(which == 4, g_lo[one, :], 0.0)))))
        lhs = jnp.where(pa[one, :] == pa_row, val, 0.0).astype(BF16)
        rhs = jnp.where(pb[one, :] == pb_row, 1.0, 0.0).astype(BF16)
        res = _dot_nt(lhs, rhs)
        part = lambda k: res[k * ca:(k + 1) * ca, :]
        idx_ref[0, e * ca:(e + 1) * ca, :] = (part(0) * 64.0 + part(1)).astype(I32)
        gate_ref[0, e * ca:(e + 1) * ca, :] = part(2) + part(3) + part(4)


def _route(aff_sets):
    n_sets, n_exp, n = aff_sets.shape
    cap = EC_CAPACITY * n // n_exp
    cb = min(32, cap)
    assert cap % cb == 0 and cb & (cb - 1) == 0
    spec = pl.BlockSpec((1, n_exp * cap // cb, cb), lambda s: (s, 0, 0))
    idx, gate = pl.pallas_call(
        functools.partial(_route_kernel, n=n, cap=cap, n_exp=n_exp, cb=cb),
        grid=(n_sets,),
        in_specs=[pl.BlockSpec((1, n_exp, n), lambda s: (s, 0, 0))],
        out_specs=[spec, spec],
        out_shape=[jax.ShapeDtypeStruct((n_sets, n_exp * cap // cb, cb), I32),
                   jax.ShapeDtypeStruct((n_sets, n_exp * cap // cb, cb), F32)],
        compiler_params=_params(("arbitrary",)),
        name="route",
    )(aff_sets)
    return idx.reshape(n_sets, n_exp, cap), gate.reshape(n_sets, n_exp, cap)


def _ffn_kernel(idx_ref, hpk_hbm, gate_ref, g2_ref, wg_ref, wu_ref, wd_ref, x_in_hbm, x_hbm,
                hpk, hb, yacc, xbuf, sem_h, sem_x, sem_s, *, m, n_f, n_groups, groups, n_split, per, ctx_seg):
    del x_in_hbm
    e = pl.program_id(0)
    part = pl.program_id(1)
    f = pl.program_id(2)
    g = e * n_split + part
    slot = g % 2
    g_prev = jnp.maximum(g - 1, 0)
    g_next = (g + 1) % n_groups
    half = n_f // 2
    rows_rw = m // half
    h_steps = max(n_f - 1, 1)
    h_base = -(-m // (h_steps * SUBLANES)) * SUBLANES
    h_parts = [(min(k * h_base, m), min((k + 1) * h_base, m)) for k in range(h_steps)]

    def h_copy(r, j, dst_slot):
        return pltpu.make_async_copy(hpk_hbm.at[pl.ds(r, 1), :], hpk.at[dst_slot, pl.ds(j, 1), :],
                                     sem_h.at[dst_slot])

    def x_copy(r, j):
        return pltpu.make_async_copy(x_hbm.at[pl.ds(r, 1), :], xbuf.at[pl.ds(j, 1), :], sem_x)

    def s_copy(r, j):
        return pltpu.make_async_copy(xbuf.at[pl.ds(j, 1), :], x_hbm.at[pl.ds(r, 1), :], sem_s)

    def start_rows(copy, grp, first, n):
        for j in range(n):
            copy(idx_ref[grp * m + first + j], first + j).start()

    def wait_h(s):
        pltpu.make_async_copy(hpk.at[1 - s], hpk.at[s], sem_h.at[s]).wait()

    def wait_rows(sem):
        pltpu.make_async_copy(yacc, xbuf, sem).wait()

    def for_rows(fn):
        def body(j, carry):
            fn(j)
            return carry
        lax.fori_loop(0, m, body, 0)

    @pl.when((g == 0) & (f == 0))
    def _():
        for_rows(lambda j: h_copy(idx_ref[j], j, 0).start())
        for_rows(lambda j: x_copy(idx_ref[j], j).start())
        wait_rows(sem_x)

    @pl.when(f == 0)
    def _():
        wait_h(slot)
        p = hpk[slot]
        d_half = hpk.shape[2]
        hb[:, :d_half] = lax.bitcast_convert_type(p << 16, F32).astype(BF16)
        hb[:, d_half:] = lax.bitcast_convert_type(p & jnp.uint32(0xFFFF0000), F32).astype(BF16)

    @pl.when(f == half)
    def _():
        wait_rows(sem_s)

    def matmuls(first):
        h = hb[...]
        a = jnp.dot(h, wg_ref[...].astype(BF16), preferred_element_type=F32)
        u = jnp.dot(h, wu_ref[...].astype(BF16), preferred_element_type=F32)
        z = (a / (1.0 + jnp.exp(-a)) * u).astype(BF16)
        y = jnp.dot(z, wd_ref[...].astype(BF16), preferred_element_type=F32)
        if first:
            yacc[...] = y
        else:
            yacc[...] += y

    for step in range(n_f):
        @pl.when(f == step)
        def _(step=step):
            if step < half:
                start_rows(s_copy, g_prev, step * rows_rw, rows_rw)
            else:
                start_rows(x_copy, g, (step - half) * rows_rw, rows_rw)
            if step < h_steps:
                start_rows(lambda r, j: h_copy(r, j, 1 - slot), g_next, h_parts[step][0],
                           h_parts[step][1] - h_parts[step][0])
            matmuls(first=step == 0)

    @pl.when(f == n_f - 1)
    def _():
        wait_rows(sem_x)
        for off, size, sample in groups:
            seg = ctx_seg if sample is None else part * per + sample
            rows = pl.ds(off, size)
            xbuf[rows, :] = xbuf[rows, :] + g2_ref[pl.ds(seg, 1), :] * (gate_ref[0, 0, rows, :] * yacc[rows, :])

    @pl.when((g == n_groups - 1) & (f == n_f - 1))
    def _():
        for_rows(lambda j: s_copy(idx_ref[g * m + j], j).start())
        wait_rows(sem_s)
        wait_h(1 - slot)


def _experts(idx_flat, gate, hpk, g2tab, w_gate, w_up, w_down, xs, layer, m, groups, n_split, per, ctx_seg):
    d = xs.shape[1]
    n_exp, d_exp = w_gate.shape[1], w_gate.shape[3]
    tf = _pick_tile(256, d_exp // 2)
    n_f = d_exp // tf
    n_groups = n_exp * n_split
    assert n_f % 2 == 0 and m % (n_f // 2) == 0 and n_groups % 2 == 0
    grid_spec = pltpu.PrefetchScalarGridSpec(
        num_scalar_prefetch=1,
        grid=(n_exp, n_split, n_f),
        in_specs=[
            pl.BlockSpec(memory_space=pl.ANY),
            pl.BlockSpec((1, 1, m, 1), lambda e, p, f, idx: (e, p, 0, 0)),
            pl.BlockSpec(g2tab.shape, lambda e, p, f, idx: (0, 0)),
            pl.BlockSpec((None, None, d, tf), lambda e, p, f, idx: (layer, e, 0, f)),
            pl.BlockSpec((None, None, d, tf), lambda e, p, f, idx: (layer, e, 0, f)),
            pl.BlockSpec((None, None, tf, d), lambda e, p, f, idx: (layer, e, f, 0)),
            pl.BlockSpec(memory_space=pl.ANY),
        ],
        out_specs=pl.BlockSpec(memory_space=pl.ANY),
        scratch_shapes=[pltpu.VMEM((2, m, d // 2), U32), pltpu.VMEM((m, d), BF16), pltpu.VMEM((m, d), F32),
                        pltpu.VMEM((m, d), F32), pltpu.SemaphoreType.DMA((2,)), pltpu.SemaphoreType.DMA,
                        pltpu.SemaphoreType.DMA],
    )
    return pl.pallas_call(
        functools.partial(_ffn_kernel, m=m, n_f=n_f, n_groups=n_groups, groups=tuple(groups), n_split=n_split,
                          per=per, ctx_seg=ctx_seg),
        grid_spec=grid_spec,
        out_shape=jax.ShapeDtypeStruct(xs.shape, xs.dtype),
        input_output_aliases={7: 0},
        compiler_params=_params(("arbitrary",) * 3),
        name="experts",
    )(idx_flat, hpk, gate, g2tab, w_gate, w_up, w_down, xs)


def _rope_tables(seq):
    pos = np.arange(seq)
    n_freq = HEAD_DIM // 4
    inv = jnp.asarray(ROPE_THETA, F32) ** (-jnp.arange(n_freq, dtype=F32) / n_freq)
    row = jnp.asarray(pos // GRID_W, F32)[:, None] * inv
    col = jnp.asarray(pos % GRID_W, F32)[:, None] * inv
    zero = jnp.zeros_like(row)
    cos = jnp.concatenate([jnp.cos(row), jnp.cos(row), jnp.cos(col), jnp.cos(col)], axis=-1)
    s1 = jnp.concatenate([zero, jnp.sin(row), zero, jnp.sin(col)], axis=-1)
    s2 = jnp.concatenate([-jnp.sin(row), zero, -jnp.sin(col), zero], axis=-1)

    def padded(tm):
        ident = jnp.ones((tm, HEAD_DIM), F32)
        none = jnp.zeros((tm, HEAD_DIM), F32)
        return (jnp.concatenate([cos, ident], axis=0), jnp.concatenate([s1, none], axis=0),
                jnp.concatenate([s2, none], axis=0))

    return padded


def _expert_rows(idx_l, gate_l, idx_c, gate_c, dims, n_split):
    n_batch, seq, ctx_len, _ = dims
    per = n_batch // n_split
    n_exp = idx_l.shape[1]

    def arrange(v):
        s = v.shape[-1]
        return v.reshape(n_split, per, n_exp, s).transpose(2, 0, 1, 3).reshape(n_exp, n_split, per * s)

    rows_l = idx_l + (jnp.arange(n_batch, dtype=I32) * seq)[:, None, None]
    parts_i, parts_g = [arrange(rows_l)], [arrange(gate_l)]
    groups = [(b * idx_l.shape[-1], idx_l.shape[-1], b) for b in range(per)]
    if idx_c is not None:
        rows_c = idx_c + (n_batch * seq + jnp.arange(n_batch, dtype=I32) * ctx_len)[:, None, None]
        parts_i.append(arrange(rows_c))
        parts_g.append(arrange(gate_c))
        groups += [(per * idx_l.shape[-1], per * idx_c.shape[-1], None)]
    idx = jnp.concatenate(parts_i, axis=-1)
    gate = jnp.concatenate(parts_g, axis=-1)
    return idx, gate, groups


def kernel(x, c, ctx, c_ctx, w_mod, b_mod, norm1_g, norm2_g, w_in, qn_a, kn_a, sink_a, vn_b, w_s, b_s,
           qn_c, kn_c, out_g, w_out, w_router, w_gate, w_up, w_down):
    n_batch, seq, d = x.shape
    ctx_len = ctx.shape[1]
    depth = w_mod.shape[0]
    n_exp = w_router.shape[2]
    dims = (n_batch, seq, ctx_len, d)
    t_lat = n_batch * seq
    assert n_batch < MOD_ROWS and seq % WINDOW == 0 and ctx_len % B_CHUNK == 0 and t_lat % ctx_len == 0
    n_split = 2 if n_batch % 2 == 0 else 1
    per = n_batch // n_split

    xs = (x.reshape(t_lat, d), ctx.reshape(n_batch * ctx_len, d))
    cvec = jnp.zeros((MOD_ROWS, d), F32).at[:n_batch].set(c).at[n_batch].set(c_ctx)
    rope = _rope_tables(seq)

    for layer in range(depth):
        last = layer == depth - 1
        mod = _modulation(cvec, w_mod, b_mod, layer)
        modt = mod.reshape(MOD_ROWS * N_MOD, 1, d)
        g2tab = mod.reshape(MOD_ROWS, N_MOD, d)[:, N_MOD - 1, :]
        row = lambda v: v[layer].reshape(1, -1)
        bsb = jnp.broadcast_to(b_s[layer][:, :, None], (B_GROUPS, B_CHUNK, HEAD_DIM))
        qa, ka, va, qc, kc, vc, o_b = _inproj(
            xs, modt, row(norm1_g), w_in[layer].astype(BF16), rope, row(qn_a), row(kn_a), row(qn_c),
            row(kn_c), row(vn_b), w_s[layer].astype(BF16), bsb, dims)
        o_a = _attn_win(sink_a[layer], qa, ka, va, dims)
        o_c = _attn_glob(qc, kc, vc, dims)
        o_ax = o_cx = None
        if not last:
            o_ax = _attn_ctx(sink_a[layer], qa, ka, va, A_HEADS, A_KV, True, dims)
            o_cx = _attn_ctx(sink_a[layer], qc, kc, vc, C_HEADS, C_KV, False, dims)
        rows_out = t_lat if last else t_lat + n_batch * ctx_len
        xs, h2, aff = _merge(o_a, o_c, o_ax, o_cx, o_b, row(out_g), w_out[layer].astype(BF16), xs, modt,
                             row(norm2_g), w_router[layer], rows_out, dims)

        idx_l, gate_l = _route(aff[:t_lat].reshape(n_batch, seq, n_exp).transpose(0, 2, 1))
        idx_c = gate_c = None
        if not last:
            idx_c, gate_c = _route(aff[t_lat:].reshape(n_batch, ctx_len, n_exp).transpose(0, 2, 1))
        idx, gate, groups = _expert_rows(idx_l, gate_l, idx_c, gate_c, dims, n_split)
        m = idx.shape[-1]
        xs = _experts(idx.reshape(-1), gate.reshape(n_exp, n_split, m, 1), h2, g2tab, w_gate, w_up, w_down,
                      xs, layer, m, groups, n_split, per, n_batch)
    return xs.reshape(n_batch, seq, d)
```

```python
import functools

import jax
import jax.numpy as jnp
import numpy as np
from jax import lax
from jax.experimental import pallas as pl
from jax.experimental.pallas import tpu as pltpu

F32 = jnp.float32
BF16 = jnp.bfloat16
I32 = jnp.int32
U32 = jnp.uint32

HEAD_DIM = 128
GRID_W = 64
WINDOW = 128
A_HEADS, A_KV = 6, 2
B_GROUPS, B_CHUNK = 4, 128
C_HEADS, C_KV = 6, 2
A_WIDTH = A_HEADS * HEAD_DIM
B_WIDTH = B_GROUPS * HEAD_DIM
C_WIDTH = C_HEADS * HEAD_DIM
KV_WIDTH = A_KV * HEAD_DIM
OFF_QA = 0
OFF_KA = OFF_QA + A_WIDTH
OFF_VA = OFF_KA + KV_WIDTH
OFF_UB = OFF_VA + KV_WIDTH
OFF_VB = OFF_UB + B_WIDTH
OFF_QC = OFF_VB + B_WIDTH
OFF_KC = OFF_QC + C_WIDTH
OFF_VC = OFF_KC + KV_WIDTH
IN_WIDTH = OFF_VC + KV_WIDTH
EC_CAPACITY = 2
N_MOD = 6
ROPE_THETA = 10000.0
EPS = 1e-6
NEG_INF = -1e30
LOG2E = 1.4426950408889634
QK_SCALE = HEAD_DIM ** -0.5 * LOG2E
SUBLANES = 8
PIPELINE_DEPTH = 2
MOD_ROWS = 8

VMEM_LIMIT = 56 * 1024 * 1024

def _params(sem, vmem=VMEM_LIMIT):
    return pltpu.CompilerParams(dimension_semantics=sem, vmem_limit_bytes=vmem)


def _pick_tile(target, *sizes):
    t = target
    while any(s % t for s in sizes):
        t //= 2
    return t


def _rms_scale(x):
    return x * lax.rsqrt(jnp.mean(x * x, axis=-1, keepdims=True) + EPS)


def _dot_nt(a, b):
    return lax.dot_general(a, b, (((1,), (1,)), ((), ())), preferred_element_type=F32)


def _mod_kernel(c_ref, w_ref, b_ref, o_ref):
    c = c_ref[...]
    s = c / (1.0 + jnp.exp(-c))
    o_ref[...] = jnp.dot(s, w_ref[...], precision=lax.Precision.HIGHEST,
                         preferred_element_type=F32) + b_ref[...]


def _modulation(cvec, w_mod, b_mod, layer):
    d = cvec.shape[1]
    n = w_mod.shape[2]
    tn = _pick_tile(1024, n)
    return pl.pallas_call(
        _mod_kernel,
        grid=(n // tn,),
        in_specs=[
            pl.BlockSpec((MOD_ROWS, d), lambda j: (0, 0)),
            pl.BlockSpec((None, d, tn), lambda j: (layer, 0, j)),
            pl.BlockSpec((None, 1, tn), lambda j: (layer, 0, j)),
        ],
        out_specs=pl.BlockSpec((MOD_ROWS, tn), lambda j: (0, j)),
        out_shape=jax.ShapeDtypeStruct((MOD_ROWS, n), F32),
        compiler_params=_params(("arbitrary",)),
        name="modulation",
    )(cvec, w_mod, b_mod.reshape(b_mod.shape[0], 1, n))


def _inproj_kernel(*refs, tm, n_lat_tiles, split_input, sub):
    if split_input:
        xl_ref, xc_ref = refs[:2]
        refs = refs[2:]
    else:
        xl_ref = refs[0]
        refs = refs[1:]
    (sh_ref, sc_ref, g_ref, w_ref, cos_ref, s1_ref, s2_ref, qna_ref, kna_ref, qnc_ref, knc_ref, vnb_ref, ws_ref,
     bsb_ref, qa_ref, ka_ref, va_ref, qc_ref, kc_ref, vc_ref, ob_ref) = refs

    for r0 in range(0, tm, sub):
        rows = slice(r0, r0 + sub)
        x = xl_ref[rows, :]
        if split_input:
            x = jnp.where(pl.program_id(0) < n_lat_tiles, x, xc_ref[rows, :])
        h = _rms_scale(x) * g_ref[...]
        h = h * (1.0 + sc_ref[0]) + sh_ref[0]
        hb = h.astype(BF16)
        cos, s1, s2 = cos_ref[rows, :], s1_ref[rows, :], s2_ref[rows, :]

        def proj(off, width):
            return jnp.dot(hb, w_ref[:, off:off + width], preferred_element_type=F32)

        def qk_head(p, gain, scale):
            q = _rms_scale(p) * gain
            q = q * cos + pltpu.roll(q, 32, 1) * s1 + pltpu.roll(q, 96, 1) * s2
            if scale != 1.0:
                q = q * scale
            return q.astype(BF16)

        def heads(off, n_heads, gain_ref, scale, out_ref):
            p = proj(off, n_heads * HEAD_DIM)
            for hd in range(n_heads):
                sl = slice(hd * HEAD_DIM, (hd + 1) * HEAD_DIM)
                out_ref[rows, sl] = qk_head(p[:, sl], gain_ref[...], scale)

        pu = proj(OFF_UB, B_WIDTH)
        pv = proj(OFF_VB, B_WIDTH)
        heads(OFF_QA, A_HEADS, qna_ref, QK_SCALE, qa_ref)
        heads(OFF_KA, A_KV, kna_ref, 1.0, ka_ref)
        va_ref[rows, :] = proj(OFF_VA, KV_WIDTH).astype(BF16)
        heads(OFF_QC, C_HEADS, qnc_ref, QK_SCALE, qc_ref)
        heads(OFF_KC, C_KV, knc_ref, 1.0, kc_ref)
        vc_ref[rows, :] = proj(OFF_VC, KV_WIDTH).astype(BF16)

        for g in range(B_GROUPS):
            sl = slice(g * HEAD_DIM, (g + 1) * HEAD_DIM)
            u = jax.nn.gelu(pu[:, sl])
            vn = (_rms_scale(jax.nn.gelu(pv[:, sl])) * vnb_ref[:, sl]).astype(BF16)
            for c in range(sub // B_CHUNK):
                chunk = slice(c * B_CHUNK, (c + 1) * B_CHUNK)
                mixed = jnp.dot(ws_ref[g], vn[chunk, :], preferred_element_type=F32) + bsb_ref[g]
                ob_ref[r0 + c * B_CHUNK:r0 + (c + 1) * B_CHUNK, sl] = u[chunk, :] * mixed


def _row_seg(i, n_lat_tiles, tiles_per_seq, n_batch):
    return jnp.where(i < n_lat_tiles, i // tiles_per_seq, n_batch)


def _inproj(xs, modt, norm_g, w_in_bf, rope, qn_a, kn_a, qn_c, kn_c, vn_b, ws_bf, bsb, dims):
    n_batch, seq, ctx_len, d = dims
    split_input = isinstance(xs, tuple)
    t_rows = n_batch * (seq + ctx_len)
    tm = _pick_tile(512, seq, n_batch * ctx_len)
    n_lat_tiles = n_batch * seq // tm
    tps = seq // tm
    cos_t, s1_t, s2_t = rope(tm)
    if split_input:
        x_specs = [pl.BlockSpec((tm, d), lambda i: (jnp.minimum(i, n_lat_tiles - 1), 0)),
                   pl.BlockSpec((tm, d), lambda i: (jnp.maximum(i - n_lat_tiles, 0), 0))]
        x_args = list(xs)
    else:
        x_specs = [pl.BlockSpec((tm, d), lambda i: (i, 0))]
        x_args = [xs]

    def mod_spec(k):
        return pl.BlockSpec((1, 1, d), lambda i: (_row_seg(i, n_lat_tiles, tps, n_batch) * N_MOD + k, 0, 0))

    def rope_spec():
        return pl.BlockSpec((tm, HEAD_DIM), lambda i: (jnp.where(i < n_lat_tiles, i % tps, tps), 0))

    def full(shape):
        return pl.BlockSpec(shape, lambda i: (0,) * len(shape))

    def out(width, dtype):
        return pl.BlockSpec((tm, width), lambda i: (i, 0)), jax.ShapeDtypeStruct((t_rows, width), dtype)

    outs = [out(A_WIDTH, BF16), out(KV_WIDTH, BF16), out(KV_WIDTH, BF16),
            out(C_WIDTH, BF16), out(KV_WIDTH, BF16), out(KV_WIDTH, BF16), out(B_WIDTH, F32)]
    return pl.pallas_call(
        functools.partial(_inproj_kernel, tm=tm, n_lat_tiles=n_lat_tiles, split_input=split_input,
                          sub=min(256, tm)),
        grid=(t_rows // tm,),
        in_specs=x_specs + [
            mod_spec(0), mod_spec(1),
            full((1, d)),
            full((d, IN_WIDTH)),
            rope_spec(), rope_spec(), rope_spec(),
            full((1, HEAD_DIM)), full((1, HEAD_DIM)), full((1, HEAD_DIM)), full((1, HEAD_DIM)),
            full((1, B_WIDTH)),
            full((B_GROUPS, B_CHUNK, B_CHUNK)),
            full((B_GROUPS, B_CHUNK, HEAD_DIM)),
        ],
        out_specs=[o[0] for o in outs],
        out_shape=[o[1] for o in outs],
        compiler_params=_params(("arbitrary",)),
        name="inproj",
    )(*x_args, modt, modt, norm_g, w_in_bf, cos_t, s1_t, s2_t, qn_a, kn_a, qn_c, kn_c, vn_b, ws_bf, bsb)


def _softmax_pv(s_list, v_list, sink):
    m = s_list[0].max(axis=-1, keepdims=True)
    for s in s_list[1:]:
        m = jnp.maximum(m, s.max(axis=-1, keepdims=True))
    if sink is not None:
        m = jnp.maximum(m, sink)
    l = None
    o = None
    for s, v in zip(s_list, v_list):
        p = jnp.exp2(s - m)
        ls = jnp.sum(p, axis=-1, keepdims=True)
        os_ = jnp.dot(p.astype(BF16), v, preferred_element_type=F32)
        l = ls if l is None else l + ls
        o = os_ if o is None else o + os_
    if sink is not None:
        l = l + jnp.exp2(sink - m)
    return o / l


def _attn_win_kernel(sink_ref, q_ref, kx_ref, kp_ref, km_ref, kn_ref, vx_ref, vp_ref, vm_ref, vn_ref,
                     o_ref, s_ref, *, tq, seq, n_g):
    i = pl.program_id(1)
    kvh = pl.program_id(2)
    kband = jnp.concatenate([kp_ref[...], km_ref[...], kn_ref[...]], axis=0)
    vband = jnp.concatenate([vp_ref[...], vm_ref[...], vn_ref[...]], axis=0)
    kx, vx = kx_ref[...], vx_ref[...]
    n_ctx = kx.shape[0]
    n_slots = s_ref.shape[0]
    nb = 3 * WINDOW
    rows_all = n_g * WINDOW
    r = lax.broadcasted_iota(I32, (rows_all, nb), 0) % WINDOW
    c = lax.broadcasted_iota(I32, (rows_all, nb), 1)
    head = lax.broadcasted_iota(I32, (rows_all, 1), 0) // WINDOW
    sink = sink_ref[kvh * n_g] * LOG2E
    for g in range(1, n_g):
        sink = jnp.where(head == g, sink_ref[kvh * n_g + g] * LOG2E, sink)

    def scores(j):
        rows = slice(j * WINDOW, (j + 1) * WINDOW)
        q = jnp.concatenate([q_ref[rows, g * HEAD_DIM:(g + 1) * HEAD_DIM] for g in range(n_g)], axis=0)
        kpos = c + (i * tq + (j - 1) * WINDOW)
        valid = (jnp.abs(c - WINDOW - r) <= WINDOW) & (kpos >= 0) & (kpos < seq)
        s_ref[j % n_slots, :, :n_ctx] = _dot_nt(q, kx)
        s_ref[j % n_slots, :, n_ctx:] = jnp.where(valid, _dot_nt(q, kband[j * WINDOW:j * WINDOW + nb, :]), NEG_INF)

    n_blocks = tq // WINDOW
    for j in range(min(PIPELINE_DEPTH, n_blocks)):
        scores(j)
    for j in range(n_blocks):
        if j + PIPELINE_DEPTH < n_blocks:
            scores(j + PIPELINE_DEPTH)
        slot = j % n_slots
        m = jnp.maximum(s_ref[slot].max(axis=-1, keepdims=True), sink)
        p_x = jnp.exp2(s_ref[slot, :, :n_ctx] - m)
        p_b = jnp.exp2(s_ref[slot, :, n_ctx:] - m)
        l = jnp.sum(p_x, axis=-1, keepdims=True) + jnp.sum(p_b, axis=-1, keepdims=True) + jnp.exp2(sink - m)
        o = (jnp.dot(p_x.astype(BF16), vx, preferred_element_type=F32)
             + jnp.dot(p_b.astype(BF16), vband[j * WINDOW:j * WINDOW + nb, :], preferred_element_type=F32)) / l
        for g in range(n_g):
            o_ref[j * WINDOW:(j + 1) * WINDOW, g * HEAD_DIM:(g + 1) * HEAD_DIM] = o[g * WINDOW:(g + 1) * WINDOW, :]


def _attn_win(sink, qa, ka, va, dims):
    n_batch, seq, ctx_len, _ = dims
    t_rows = n_batch * seq
    tq = _pick_tile(1024, seq)
    nq = seq // tq
    n_g = A_HEADS // A_KV
    wb = tq // WINDOW
    sb = seq // WINDOW
    lat_ctx_blocks = n_batch * seq // ctx_len

    def q_map(b, i, k):
        return (b * nq + i, k)

    def prev_map(b, i, k):
        return (b * sb + jnp.maximum(i * wb - 1, 0), k)

    def next_map(b, i, k):
        return (b * sb + jnp.minimum((i + 1) * wb, sb - 1), k)

    def ctx_map(b, i, k):
        return (lat_ctx_blocks + b, k)

    kv_specs = [pl.BlockSpec((ctx_len, HEAD_DIM), ctx_map), pl.BlockSpec((WINDOW, HEAD_DIM), prev_map),
                pl.BlockSpec((tq, HEAD_DIM), q_map), pl.BlockSpec((WINDOW, HEAD_DIM), next_map)]
    return pl.pallas_call(
        functools.partial(_attn_win_kernel, tq=tq, seq=seq, n_g=n_g),
        grid=(n_batch, nq, A_KV),
        in_specs=[pl.BlockSpec(memory_space=pltpu.SMEM),
                  pl.BlockSpec((tq, n_g * HEAD_DIM), q_map)] + kv_specs + kv_specs,
        out_specs=pl.BlockSpec((tq, n_g * HEAD_DIM), q_map),
        out_shape=jax.ShapeDtypeStruct((t_rows, A_WIDTH), F32),
        scratch_shapes=[pltpu.VMEM((PIPELINE_DEPTH + 1, n_g * WINDOW, ctx_len + 3 * WINDOW), F32)],
        compiler_params=_params(("arbitrary",) * 3),
        name="attn_window",
    )(sink, qa, ka, ka, ka, ka, va, va, va, va)


def _attn_glob_kernel(q_ref, kx_ref, kl_ref, vx_ref, vl_ref, o_ref, s_ref, *, n_g, sub):
    kx, kl, vx, vl = kx_ref[...], kl_ref[...], vx_ref[...], vl_ref[...]
    n_ctx = kx.shape[0]
    chains = [(slice(r0, r0 + sub), slice(g * HEAD_DIM, (g + 1) * HEAD_DIM))
              for r0 in range(0, q_ref.shape[0], sub) for g in range(n_g)]
    n_slots = s_ref.shape[0]

    def scores(i):
        q = q_ref[chains[i][0], chains[i][1]]
        s_ref[i % n_slots, :, :n_ctx] = _dot_nt(q, kx)
        s_ref[i % n_slots, :, n_ctx:] = _dot_nt(q, kl)

    for i in range(min(PIPELINE_DEPTH, len(chains))):
        scores(i)
    for i, chain in enumerate(chains):
        if i + PIPELINE_DEPTH < len(chains):
            scores(i + PIPELINE_DEPTH)
        slot = i % n_slots
        m = s_ref[slot].max(axis=-1, keepdims=True)
        p_x = jnp.exp2(s_ref[slot, :, :n_ctx] - m)
        p_l = jnp.exp2(s_ref[slot, :, n_ctx:] - m)
        l = jnp.sum(p_x, axis=-1, keepdims=True) + jnp.sum(p_l, axis=-1, keepdims=True)
        o = (jnp.dot(p_x.astype(BF16), vx, preferred_element_type=F32)
             + jnp.dot(p_l.astype(BF16), vl, preferred_element_type=F32))
        o_ref[chain[0], chain[1]] = o / l


def _attn_glob(qc, kc, vc, dims):
    n_batch, seq, ctx_len, _ = dims
    t_rows = n_batch * seq
    tq = _pick_tile(512, seq)
    sub = min(128, tq)
    nq = seq // tq
    n_g = C_HEADS // C_KV
    lat_ctx_blocks = n_batch * seq // ctx_len

    def q_map(b, k, i):
        return (b * nq + i, k)

    kv_specs = [pl.BlockSpec((ctx_len, HEAD_DIM), lambda b, k, i: (lat_ctx_blocks + b, k)),
                pl.BlockSpec((seq, HEAD_DIM), lambda b, k, i: (b, k))]
    return pl.pallas_call(
        functools.partial(_attn_glob_kernel, n_g=n_g, sub=sub),
        grid=(n_batch, C_KV, nq),
        in_specs=[pl.BlockSpec((tq, n_g * HEAD_DIM), q_map)] + kv_specs + kv_specs,
        out_specs=pl.BlockSpec((tq, n_g * HEAD_DIM), q_map),
        out_shape=jax.ShapeDtypeStruct((t_rows, C_WIDTH), F32),
        scratch_shapes=[pltpu.VMEM((PIPELINE_DEPTH + 1, sub, ctx_len + seq), F32)],
        compiler_params=_params(("arbitrary",) * 3),
        name="attn_global",
    )(qc, kc, kc, vc, vc)


def _attn_ctx_kernel(sink_ref, q_ref, k_ref, v_ref, o_ref, *, n_g, use_sink):
    kvh = pl.program_id(1)
    k, v = k_ref[...], v_ref[...]
    for g in range(n_g):
        sl = slice(g * HEAD_DIM, (g + 1) * HEAD_DIM)
        sink = sink_ref[kvh * n_g + g] * LOG2E if use_sink else None
        o_ref[:, sl] = _softmax_pv([_dot_nt(q_ref[:, sl], k)], [v], sink)


def _attn_ctx(sink, q, k, v, n_heads, n_kv, use_sink, dims):
    n_batch, seq, ctx_len, _ = dims
    n_g = n_heads // n_kv
    lat_ctx_blocks = n_batch * seq // ctx_len

    def blk(b, k):
        return (lat_ctx_blocks + b, k)

    return pl.pallas_call(
        functools.partial(_attn_ctx_kernel, n_g=n_g, use_sink=use_sink),
        grid=(n_batch, n_kv),
        in_specs=[pl.BlockSpec(memory_space=pltpu.SMEM),
                  pl.BlockSpec((ctx_len, n_g * HEAD_DIM), blk),
                  pl.BlockSpec((ctx_len, HEAD_DIM), blk),
                  pl.BlockSpec((ctx_len, HEAD_DIM), blk)],
        out_specs=pl.BlockSpec((ctx_len, n_g * HEAD_DIM), lambda b, k: (b, k)),
        out_shape=jax.ShapeDtypeStruct((n_batch * ctx_len, n_heads * HEAD_DIM), F32),
        compiler_params=_params(("arbitrary",) * 2),
        name="attn_context",
    )(sink, q, k, v)


def _merge_kernel(*refs, n_lat_tiles, has_ctx, split_x, sub):
    if has_ctx:
        oa_ref, oc_ref, oax_ref, ocx_ref = refs[:4]
        refs = refs[4:]
    else:
        oa_ref, oc_ref = refs[:2]
        refs = refs[2:]
    if split_x:
        x_ref, xc_ref = refs[:2]
        refs = refs[2:]
    else:
        x_ref = refs[0]
        refs = refs[1:]
    ob_ref, og_ref, w_ref, g1_ref, sh_ref, sc_ref, n2_ref, wr_ref, xo_ref, h_ref, aff_ref = refs
    is_lat = pl.program_id(0) < n_lat_tiles
    n_exp = aff_ref.shape[1]
    half = x_ref.shape[1] // 2

    def group(o, off, width):
        y = (_rms_scale(o) * og_ref[:, off:off + width]).astype(BF16)
        return jnp.dot(y, w_ref[off:off + width, :], preferred_element_type=F32)

    for r0 in range(0, x_ref.shape[0], sub):
        rows = slice(r0, r0 + sub)
        o_a, o_c = oa_ref[rows, :], oc_ref[rows, :]
        if has_ctx:
            o_a = jnp.where(is_lat, o_a, oax_ref[rows, :])
            o_c = jnp.where(is_lat, o_c, ocx_ref[rows, :])
        y = (group(o_a, 0, A_WIDTH) + group(ob_ref[rows, :], A_WIDTH, B_WIDTH)
             + group(o_c, A_WIDTH + B_WIDTH, C_WIDTH))
        x = x_ref[rows, :]
        if split_x:
            x = jnp.where(is_lat, x, xc_ref[rows, :])
        xn = x + g1_ref[0] * y
        xo_ref[rows, :] = xn
        h = _rms_scale(xn) * n2_ref[...]
        h = h * (1.0 + sc_ref[0]) + sh_ref[0]
        hb = h.astype(BF16)
        lo = lax.bitcast_convert_type(hb[:, :half].astype(F32), U32) >> 16
        hi = lax.bitcast_convert_type(hb[:, half:].astype(F32), U32) & jnp.uint32(0xFFFF0000)
        h_ref[rows, :] = lo | hi
        h_lo = (h - hb.astype(F32)).astype(BF16)
        both = jnp.dot(hb, wr_ref[...], preferred_element_type=F32)
        logits = (both[:, :n_exp] + both[:, n_exp:]
                  + jnp.dot(h_lo, wr_ref[:, :n_exp], preferred_element_type=F32))
        e = jnp.exp(logits - logits.max(axis=-1, keepdims=True))
        aff_ref[rows, :] = e / jnp.sum(e, axis=-1, keepdims=True)


def _merge(o_a, o_c, o_ax, o_cx, o_b, out_g, w_out_bf, xs, modt, norm2_g, w_router, rows_out, dims):
    n_batch, seq, ctx_len, d = dims
    tm = _pick_tile(512, seq, n_batch * ctx_len)
    n_lat_tiles = n_batch * seq // tm
    tps = seq // tm
    n_exp = w_router.shape[1]
    mix = A_WIDTH + B_WIDTH + C_WIDTH
    has_ctx = o_ax is not None
    assert has_ctx == (rows_out > n_batch * seq)
    split_x = isinstance(xs, tuple)
    assert has_ctx or not split_x
    w_hi = w_router.astype(BF16)
    w_lo = (w_router - w_hi.astype(F32)).astype(BF16)

    def mod_spec(k):
        return pl.BlockSpec((1, 1, d), lambda i: (_row_seg(i, n_lat_tiles, tps, n_batch) * N_MOD + k, 0, 0))

    def rows(width):
        return pl.BlockSpec((tm, width), lambda i: (i, 0))

    def lat_rows(width):
        return pl.BlockSpec((tm, width), lambda i: (jnp.minimum(i, n_lat_tiles - 1), 0))

    def ctx_rows(width):
        return pl.BlockSpec((tm, width), lambda i: (jnp.maximum(i - n_lat_tiles, 0), 0))

    def full(shape):
        return pl.BlockSpec(shape, lambda i: (0,) * len(shape))

    attn_specs = [lat_rows(A_WIDTH), lat_rows(C_WIDTH)]
    attn_args = [o_a, o_c]
    if has_ctx:
        attn_specs += [ctx_rows(A_WIDTH), ctx_rows(C_WIDTH)]
        attn_args += [o_ax, o_cx]
    if split_x:
        attn_specs += [lat_rows(d), ctx_rows(d)]
        attn_args += list(xs)
    else:
        attn_specs += [rows(d)]
        attn_args += [xs]
    return pl.pallas_call(
        functools.partial(_merge_kernel, n_lat_tiles=n_lat_tiles, has_ctx=has_ctx, split_x=split_x,
                          sub=min(256, tm)),
        grid=(rows_out // tm,),
        in_specs=attn_specs + [rows(B_WIDTH), full((1, mix)), full((mix, d)),
                               mod_spec(2), mod_spec(3), mod_spec(4), full((1, d)), full((d, 2 * n_exp))],
        out_specs=[rows(d), rows(d // 2), rows(n_exp)],
        out_shape=[jax.ShapeDtypeStruct((rows_out, d), F32), jax.ShapeDtypeStruct((rows_out, d // 2), U32),
                   jax.ShapeDtypeStruct((rows_out, n_exp), F32)],
        compiler_params=_params(("arbitrary",)),
        name="merge_router",
    )(*attn_args, o_b, out_g, w_out_bf, modt, modt, modt, norm2_g, jnp.concatenate([w_hi, w_lo], axis=1))


def _prefix_excl(mask_f32, tri):
    rows, n = mask_f32.shape
    cw = tri.shape[0]
    carry = jnp.zeros((rows, 1), F32)
    parts = []
    for c in range(n // cw):
        m = mask_f32[:, c * cw:(c + 1) * cw]
        parts.append(jnp.dot(m.astype(BF16), tri, preferred_element_type=F32) + carry)
        carry = carry + jnp.sum(m, axis=-1, keepdims=True)
    return parts[0] if len(parts) == 1 else jnp.concatenate(parts, axis=-1)


def _route_kernel(aff_ref, idx_ref, gate_ref, *, n, cap, n_exp, cb):
    a = aff_ref[0]
    bits = pltpu.bitcast(a, I32)

    def bit_step(b, thr):
        cand = thr | jnp.left_shift(jnp.int32(1), 30 - b)
        cnt = jnp.sum(jnp.where(bits >= cand, 1.0, 0.0), axis=-1, keepdims=True)
        return jnp.where(cnt >= cap, cand, thr)

    thr = lax.fori_loop(0, 31, bit_step, jnp.zeros((n_exp, 1), I32))
    gt = bits > thr
    eq = bits == thr
    cw = min(256, n)
    tri = jnp.where(lax.broadcasted_iota(I32, (cw, cw), 0) < lax.broadcasted_iota(I32, (cw, cw), 1),
                    1.0, 0.0).astype(BF16)
    need = cap - jnp.sum(jnp.where(gt, 1.0, 0.0), axis=-1, keepdims=True)
    sel = gt | (eq & (_prefix_excl(jnp.where(eq, 1.0, 0.0), tri) < need))
    pos = jnp.where(sel, _prefix_excl(jnp.where(sel, 1.0, 0.0), tri), -1.0)

    tok = lax.broadcasted_iota(I32, (1, n), 1)
    t_hi = (tok // 64).astype(F32)
    t_lo = (tok % 64).astype(F32)
    g_hi = a.astype(BF16).astype(F32)
    r1 = a - g_hi
    g_mid = r1.astype(BF16).astype(F32)
    g_lo = r1 - g_mid
    ca = cap // cb
    pa = jnp.floor(pos * (1.0 / cb))
    pb = pos - pa * cb
    n_val = 5
    rows_l = -(-n_val * ca // 16) * 16
    rowi = lax.broadcasted_iota(I32, (rows_l, n), 0)
    which = rowi // ca
    pa_row = (rowi % ca).astype(F32)
    pb_row = lax.broadcasted_iota(I32, (cb, n), 0).astype(F32)
    for e in range(n_exp):
        one = slice(e, e + 1)
        val = jnp.where(which == 0, t_hi,
              jnp.where(which == 1, t_lo,
              jnp.where(which == 2, g_hi[one, :],
              jnp.where(which == 3, g_mid[one, :],
              jnp.where(which == 4, g_lo[one, :], 0.0)))))
        lhs = jnp.where(pa[one, :] == pa_row, val, 0.0).astype(BF16)
        rhs = jnp.where(pb[one, :] == pb_row, 1.0, 0.0).astype(BF16)
        res = _dot_nt(lhs, rhs)
        part = lambda k: res[k * ca:(k + 1) * ca, :]
        idx_ref[0, e * ca:(e + 1) * ca, :] = (part(0) * 64.0 + part(1)).astype(I32)
        gate_ref[0, e * ca:(e + 1) * ca, :] = part(2) + part(3) + part(4)


def _route(aff_sets):
    n_sets, n_exp, n = aff_sets.shape
    cap = EC_CAPACITY * n // n_exp
    cb = min(32, cap)
    assert cap % cb == 0 and cb & (cb - 1) == 0
    spec = pl.BlockSpec((1, n_exp * cap // cb, cb), lambda s: (s, 0, 0))
    idx, gate = pl.pallas_call(
        functools.partial(_route_kernel, n=n, cap=cap, n_exp=n_exp, cb=cb),
        grid=(n_sets,),
        in_specs=[pl.BlockSpec((1, n_exp, n), lambda s: (s, 0, 0))],
        out_specs=[spec, spec],
        out_shape=[jax.ShapeDtypeStruct((n_sets, n_exp * cap // cb, cb), I32),
                   jax.ShapeDtypeStruct((n_sets, n_exp * cap // cb, cb), F32)],
        compiler_params=_params(("arbitrary",)),
        name="route",
    )(aff_sets)
    return idx.reshape(n_sets, n_exp, cap), gate.reshape(n_sets, n_exp, cap)


def _ffn_kernel(idx_ref, hpk_hbm, gate_ref, g2_ref, wg_ref, wu_ref, wd_ref, x_in_hbm, x_hbm,
                hpk, hb, yacc, xbuf, sem_h, sem_x, sem_s, *, m, n_f, n_groups, groups, n_split, per, ctx_seg):
    del x_in_hbm
    e = pl.program_id(0)
    part = pl.program_id(1)
    f = pl.program_id(2)
    g = e * n_split + part
    slot = g % 2
    g_prev = jnp.maximum(g - 1, 0)
    g_next = (g + 1) % n_groups
    half = n_f // 2
    rows_rw = m // half
    h_steps = max(n_f - 1, 1)
    h_base = -(-m // (h_steps * SUBLANES)) * SUBLANES
    h_parts = [(min(k * h_base, m), min((k + 1) * h_base, m)) for k in range(h_steps)]

    def h_copy(r, j, dst_slot):
        return pltpu.make_async_copy(hpk_hbm.at[pl.ds(r, 1), :], hpk.at[dst_slot, pl.ds(j, 1), :],
                                     sem_h.at[dst_slot])

    def x_copy(r, j):
        return pltpu.make_async_copy(x_hbm.at[pl.ds(r, 1), :], xbuf.at[pl.ds(j, 1), :], sem_x)

    def s_copy(r, j):
        return pltpu.make_async_copy(xbuf.at[pl.ds(j, 1), :], x_hbm.at[pl.ds(r, 1), :], sem_s)

    def start_rows(copy, grp, first, n):
        for j in range(n):
            copy(idx_ref[grp * m + first + j], first + j).start()

    def wait_h(s):
        pltpu.make_async_copy(hpk.at[1 - s], hpk.at[s], sem_h.at[s]).wait()

    def wait_rows(sem):
        pltpu.make_async_copy(yacc, xbuf, sem).wait()

    def for_rows(fn):
        def body(j, carry):
            fn(j)
            return carry
        lax.fori_loop(0, m, body, 0)

    @pl.when((g == 0) & (f == 0))
    def _():
        for_rows(lambda j: h_copy(idx_ref[j], j, 0).start())
        for_rows(lambda j: x_copy(idx_ref[j], j).start())
        wait_rows(sem_x)

    @pl.when(f == 0)
    def _():
        wait_h(slot)
        p = hpk[slot]
        d_half = hpk.shape[2]
        hb[:, :d_half] = lax.bitcast_convert_type(p << 16, F32).astype(BF16)
        hb[:, d_half:] = lax.bitcast_convert_type(p & jnp.uint32(0xFFFF0000), F32).astype(BF16)

    @pl.when(f == half)
    def _():
        wait_rows(sem_s)

    def matmuls(first):
        h = hb[...]
        a = jnp.dot(h, wg_ref[...].astype(BF16), preferred_element_type=F32)
        u = jnp.dot(h, wu_ref[...].astype(BF16), preferred_element_type=F32)
        z = (a / (1.0 + jnp.exp(-a)) * u).astype(BF16)
        y = jnp.dot(z, wd_ref[...].astype(BF16), preferred_element_type=F32)
        if first:
            yacc[...] = y
        else:
            yacc[...] += y

    for step in range(n_f):
        @pl.when(f == step)
        def _(step=step):
            if step < half:
                start_rows(s_copy, g_prev, step * rows_rw, rows_rw)
            else:
                start_rows(x_copy, g, (step - half) * rows_rw, rows_rw)
            if step < h_steps:
                start_rows(lambda r, j: h_copy(r, j, 1 - slot), g_next, h_parts[step][0],
                           h_parts[step][1] - h_parts[step][0])
            matmuls(first=step == 0)

    @pl.when(f == n_f - 1)
    def _():
        wait_rows(sem_x)
        for off, size, sample in groups:
            seg = ctx_seg if sample is None else part * per + sample
            rows = pl.ds(off, size)
            xbuf[rows, :] = xbuf[rows, :] + g2_ref[pl.ds(seg, 1), :] * (gate_ref[0, 0, rows, :] * yacc[rows, :])

    @pl.when((g == n_groups - 1) & (f == n_f - 1))
    def _():
        for_rows(lambda j: s_copy(idx_ref[g * m + j], j).start())
        wait_rows(sem_s)
        wait_h(1 - slot)


def _experts(idx_flat, gate, hpk, g2tab, w_gate, w_up, w_down, xs, layer, m, groups, n_split, per, ctx_seg):
    d = xs.shape[1]
    n_exp, d_exp = w_gate.shape[1], w_gate.shape[3]
    tf = _pick_tile(256, d_exp // 2)
    n_f = d_exp // tf
    n_groups = n_exp * n_split
    assert n_f % 2 == 0 and m % (n_f // 2) == 0 and n_groups % 2 == 0
    grid_spec = pltpu.PrefetchScalarGridSpec(
        num_scalar_prefetch=1,
        grid=(n_exp, n_split, n_f),
        in_specs=[
            pl.BlockSpec(memory_space=pl.ANY),
            pl.BlockSpec((1, 1, m, 1), lambda e, p, f, idx: (e, p, 0, 0)),
            pl.BlockSpec(g2tab.shape, lambda e, p, f, idx: (0, 0)),
            pl.BlockSpec((None, None, d, tf), lambda e, p, f, idx: (layer, e, 0, f)),
            pl.BlockSpec((None, None, d, tf), lambda e, p, f, idx: (layer, e, 0, f)),
            pl.BlockSpec((None, None, tf, d), lambda e, p, f, idx: (layer, e, f, 0)),
            pl.BlockSpec(memory_space=pl.ANY),
        ],
        out_specs=pl.BlockSpec(memory_space=pl.ANY),
        scratch_shapes=[pltpu.VMEM((2, m, d // 2), U32), pltpu.VMEM((m, d), BF16), pltpu.VMEM((m, d), F32),
                        pltpu.VMEM((m, d), F32), pltpu.SemaphoreType.DMA((2,)), pltpu.SemaphoreType.DMA,
                        pltpu.SemaphoreType.DMA],
    )
    return pl.pallas_call(
        functools.partial(_ffn_kernel, m=m, n_f=n_f, n_groups=n_groups, groups=tuple(groups), n_split=n_split,
                          per=per, ctx_seg=ctx_seg),
        grid_spec=grid_spec,
        out_shape=jax.ShapeDtypeStruct(xs.shape, xs.dtype),
        input_output_aliases={7: 0},
        compiler_params=_params(("arbitrary",) * 3),
        name="experts",
    )(idx_flat, hpk, gate, g2tab, w_gate, w_up, w_down, xs)


def _rope_tables(seq):
    pos = np.arange(seq)
    n_freq = HEAD_DIM // 4
    inv = jnp.asarray(ROPE_THETA, F32) ** (-jnp.arange(n_freq, dtype=F32) / n_freq)
    row = jnp.asarray(pos // GRID_W, F32)[:, None] * inv
    col = jnp.asarray(pos % GRID_W, F32)[:, None] * inv
    zero = jnp.zeros_like(row)
    cos = jnp.concatenate([jnp.cos(row), jnp.cos(row), jnp.cos(col), jnp.cos(col)], axis=-1)
    s1 = jnp.concatenate([zero, jnp.sin(row), zero, jnp.sin(col)], axis=-1)
    s2 = jnp.concatenate([-jnp.sin(row), zero, -jnp.sin(col), zero], axis=-1)

    def padded(tm):
        ident = jnp.ones((tm, HEAD_DIM), F32)
        none = jnp.zeros((tm, HEAD_DIM), F32)
        return (jnp.concatenate([cos, ident], axis=0), jnp.concatenate([s1, none], axis=0),
                jnp.concatenate([s2, none], axis=0))

    return padded


def _expert_rows(idx_l, gate_l, idx_c, gate_c, dims, n_split):
    n_batch, seq, ctx_len, _ = dims
    per = n_batch // n_split
    n_exp = idx_l.shape[1]

    def arrange(v):
        s = v.shape[-1]
        return v.reshape(n_split, per, n_exp, s).transpose(2, 0, 1, 3).reshape(n_exp, n_split, per * s)

    rows_l = idx_l + (jnp.arange(n_batch, dtype=I32) * seq)[:, None, None]
    parts_i, parts_g = [arrange(rows_l)], [arrange(gate_l)]
    groups = [(b * idx_l.shape[-1], idx_l.shape[-1], b) for b in range(per)]
    if idx_c is not None:
        rows_c = idx_c + (n_batch * seq + jnp.arange(n_batch, dtype=I32) * ctx_len)[:, None, None]
        parts_i.append(arrange(rows_c))
        parts_g.append(arrange(gate_c))
        groups += [(per * idx_l.shape[-1], per * idx_c.shape[-1], None)]
    idx = jnp.concatenate(parts_i, axis=-1)
    gate = jnp.concatenate(parts_g, axis=-1)
    return idx, gate, groups


def kernel(x, c, ctx, c_ctx, w_mod, b_mod, norm1_g, norm2_g, w_in, qn_a, kn_a, sink_a, vn_b, w_s, b_s,
           qn_c, kn_c, out_g, w_out, w_router, w_gate, w_up, w_down):
    n_batch, seq, d = x.shape
    ctx_len = ctx.shape[1]
    depth = w_mod.shape[0]
    n_exp = w_router.shape[2]
    dims = (n_batch, seq, ctx_len, d)
    t_lat = n_batch * seq
    assert n_batch < MOD_ROWS and seq % WINDOW == 0 and ctx_len % B_CHUNK == 0 and t_lat % ctx_len == 0
    n_split = 2 if n_batch % 2 == 0 else 1
    per = n_batch // n_split

    xs = (x.reshape(t_lat, d), ctx.reshape(n_batch * ctx_len, d))
    cvec = jnp.zeros((MOD_ROWS, d), F32).at[:n_batch].set(c).at[n_batch].set(c_ctx)
    rope = _rope_tables(seq)

    for layer in range(depth):
        last = layer == depth - 1
        mod = _modulation(cvec, w_mod, b_mod, layer)
        modt = mod.reshape(MOD_ROWS * N_MOD, 1, d)
        g2tab = mod.reshape(MOD_ROWS, N_MOD, d)[:, N_MOD - 1, :]
        row = lambda v: v[layer].reshape(1, -1)
        bsb = jnp.broadcast_to(b_s[layer][:, :, None], (B_GROUPS, B_CHUNK, HEAD_DIM))
        qa, ka, va, qc, kc, vc, o_b = _inproj(
            xs, modt, row(norm1_g), w_in[layer].astype(BF16), rope, row(qn_a), row(kn_a), row(qn_c),
            row(kn_c), row(vn_b), w_s[layer].astype(BF16), bsb, dims)
        o_a = _attn_win(sink_a[layer], qa, ka, va, dims)
        o_c = _attn_glob(qc, kc, vc, dims)
        o_ax = o_cx = None
        if not last:
            o_ax = _attn_ctx(sink_a[layer], qa, ka, va, A_HEADS, A_KV, True, dims)
            o_cx = _attn_ctx(sink_a[layer], qc, kc, vc, C_HEADS, C_KV, False, dims)
        rows_out = t_lat if last else t_lat + n_batch * ctx_len
        xs, h2, aff = _merge(o_a, o_c, o_ax, o_cx, o_b, row(out_g), w_out[layer].astype(BF16), xs, modt,
                             row(norm2_g), w_router[layer], rows_out, dims)

        idx_l, gate_l = _route(aff[:t_lat].reshape(n_batch, seq, n_exp).transpose(0, 2, 1))
        idx_c = gate_c = None
        if not last:
            idx_c, gate_c = _route(aff[t_lat:].reshape(n_batch, ctx_len, n_exp).transpose(0, 2, 1))
        idx, gate, groups = _expert_rows(idx_l, gate_l, idx_c, gate_c, dims, n_split)
        m = idx.shape[-1]
        xs = _experts(idx.reshape(-1), gate.reshape(n_exp, n_split, m, 1), h2, g2tab, w_gate, w_up, w_down,
                      xs, layer, m, groups, n_split, per, n_batch)
    return xs.reshape(n_batch, seq, d)
```

```python
import functools

import jax
import jax.numpy as jnp
import numpy as np
from jax import lax
from jax.experimental import pallas as pl
from jax.experimental.pallas import tpu as pltpu

F32 = jnp.float32
BF16 = jnp.bfloat16
I32 = jnp.int32
U32 = jnp.uint32

HEAD_DIM = 128
GRID_W = 64
WINDOW = 128
A_HEADS, A_KV = 6, 2
B_GROUPS, B_CHUNK = 4, 128
C_HEADS, C_KV = 6, 2
A_WIDTH = A_HEADS * HEAD_DIM
B_WIDTH = B_GROUPS * HEAD_DIM
C_WIDTH = C_HEADS * HEAD_DIM
KV_WIDTH = A_KV * HEAD_DIM
OFF_QA = 0
OFF_KA = OFF_QA + A_WIDTH
OFF_VA = OFF_KA + KV_WIDTH
OFF_UB = OFF_VA + KV_WIDTH
OFF_VB = OFF_UB + B_WIDTH
OFF_QC = OFF_VB + B_WIDTH
OFF_KC = OFF_QC + C_WIDTH
OFF_VC = OFF_KC + KV_WIDTH
IN_WIDTH = OFF_VC + KV_WIDTH
EC_CAPACITY = 2
N_MOD = 6
ROPE_THETA = 10000.0
EPS = 1e-6
NEG_INF = -1e30
LOG2E = 1.4426950408889634
QK_SCALE = HEAD_DIM ** -0.5 * LOG2E
SUBLANES = 8
PIPELINE_DEPTH = 2
MOD_ROWS = 8

VMEM_LIMIT = 56 * 1024 * 1024

def _params(sem, vmem=VMEM_LIMIT):
    return pltpu.CompilerParams(dimension_semantics=sem, vmem_limit_bytes=vmem)


def _pick_tile(target, *sizes):
    t = target
    while any(s % t for s in sizes):
        t //= 2
    return t


def _rms_scale(x):
    return x * lax.rsqrt(jnp.mean(x * x, axis=-1, keepdims=True) + EPS)


def _dot_nt(a, b):
    return lax.dot_general(a, b, (((1,), (1,)), ((), ())), preferred_element_type=F32)


def _mod_kernel(c_ref, w_ref, b_ref, o_ref):
    c = c_ref[...]
    s = c / (1.0 + jnp.exp(-c))
    o_ref[...] = jnp.dot(s, w_ref[...], precision=lax.Precision.HIGHEST,
                         preferred_element_type=F32) + b_ref[...]


def _modulation(cvec, w_mod, b_mod, layer):
    d = cvec.shape[1]
    n = w_mod.shape[2]
    tn = _pick_tile(1024, n)
    return pl.pallas_call(
        _mod_kernel,
        grid=(n // tn,),
        in_specs=[
            pl.BlockSpec((MOD_ROWS, d), lambda j: (0, 0)),
            pl.BlockSpec((None, d, tn), lambda j: (layer, 0, j)),
            pl.BlockSpec((None, 1, tn), lambda j: (layer, 0, j)),
        ],
        out_specs=pl.BlockSpec((MOD_ROWS, tn), lambda j: (0, j)),
        out_shape=jax.ShapeDtypeStruct((MOD_ROWS, n), F32),
        compiler_params=_params(("arbitrary",)),
        name="modulation",
    )(cvec, w_mod, b_mod.reshape(b_mod.shape[0], 1, n))


def _inproj_kernel(*refs, tm, n_lat_tiles, split_input, sub):
    if split_input:
        xl_ref, xc_ref = refs[:2]
        refs = refs[2:]
    else:
        xl_ref = refs[0]
        refs = refs[1:]
    (sh_ref, sc_ref, g_ref, w_ref, cos_ref, s1_ref, s2_ref, qna_ref, kna_ref, qnc_ref, knc_ref, vnb_ref, ws_ref,
     bsb_ref, qa_ref, ka_ref, va_ref, qc_ref, kc_ref, vc_ref, ob_ref) = refs

    for r0 in range(0, tm, sub):
        rows = slice(r0, r0 + sub)
        x = xl_ref[rows, :]
        if split_input:
            x = jnp.where(pl.program_id(0) < n_lat_tiles, x, xc_ref[rows, :])
        h = _rms_scale(x) * g_ref[...]
        h = h * (1.0 + sc_ref[0]) + sh_ref[0]
        hb = h.astype(BF16)
        cos, s1, s2 = cos_ref[rows, :], s1_ref[rows, :], s2_ref[rows, :]

        def proj(off, width):
            return jnp.dot(hb, w_ref[:, off:off + width], preferred_element_type=F32)

        def qk_head(p, gain, scale):
            q = _rms_scale(p) * gain
            q = q * cos + pltpu.roll(q, 32, 1) * s1 + pltpu.roll(q, 96, 1) * s2
            if scale != 1.0:
                q = q * scale
            return q.astype(BF16)

        def heads(off, n_heads, gain_ref, scale, out_ref):
            p = proj(off, n_heads * HEAD_DIM)
            for hd in range(n_heads):
                sl = slice(hd * HEAD_DIM, (hd + 1) * HEAD_DIM)
                out_ref[rows, sl] = qk_head(p[:, sl], gain_ref[...], scale)

        pu = proj(OFF_UB, B_WIDTH)
        pv = proj(OFF_VB, B_WIDTH)
        heads(OFF_QA, A_HEADS, qna_ref, QK_SCALE, qa_ref)
        heads(OFF_KA, A_KV, kna_ref, 1.0, ka_ref)
        va_ref[rows, :] = proj(OFF_VA, KV_WIDTH).astype(BF16)
        heads(OFF_QC, C_HEADS, qnc_ref, QK_SCALE, qc_ref)
        heads(OFF_KC, C_KV, knc_ref, 1.0, kc_ref)
        vc_ref[rows, :] = proj(OFF_VC, KV_WIDTH).astype(BF16)

        for g in range(B_GROUPS):
            sl = slice(g * HEAD_DIM, (g + 1) * HEAD_DIM)
            u = jax.nn.gelu(pu[:, sl])
            vn = (_rms_scale(jax.nn.gelu(pv[:, sl])) * vnb_ref[:, sl]).astype(BF16)
            for c in range(sub // B_CHUNK):
                chunk = slice(c * B_CHUNK, (c + 1) * B_CHUNK)
                mixed = jnp.dot(ws_ref[g], vn[chunk, :], preferred_element_type=F32) + bsb_ref[g]
                ob_ref[r0 + c * B_CHUNK:r0 + (c + 1) * B_CHUNK, sl] = u[chunk, :] * mixed


def _row_seg(i, n_lat_tiles, tiles_per_seq, n_batch):
    return jnp.where(i < n_lat_tiles, i // tiles_per_seq, n_batch)


def _inproj(xs, modt, norm_g, w_in_bf, rope, qn_a, kn_a, qn_c, kn_c, vn_b, ws_bf, bsb, dims):
    n_batch, seq, ctx_len, d = dims
    split_input = isinstance(xs, tuple)
    t_rows = n_batch * (seq + ctx_len)
    tm = _pick_tile(512, seq, n_batch * ctx_len)
    n_lat_tiles = n_batch * seq // tm
    tps = seq // tm
    cos_t, s1_t, s2_t = rope(tm)
    if split_input:
        x_specs = [pl.BlockSpec((tm, d), lambda i: (jnp.minimum(i, n_lat_tiles - 1), 0)),
                   pl.BlockSpec((tm, d), lambda i: (jnp.maximum(i - n_lat_tiles, 0), 0))]
        x_args = list(xs)
    else:
        x_specs = [pl.BlockSpec((tm, d), lambda i: (i, 0))]
        x_args = [xs]

    def mod_spec(k):
        return pl.BlockSpec((1, 1, d), lambda i: (_row_seg(i, n_lat_tiles, tps, n_batch) * N_MOD + k, 0, 0))

    def rope_spec():
        return pl.BlockSpec((tm, HEAD_DIM), lambda i: (jnp.where(i < n_lat_tiles, i % tps, tps), 0))

    def full(shape):
        return pl.BlockSpec(shape, lambda i: (0,) * len(shape))

    def out(width, dtype):
        return pl.BlockSpec((tm, width), lambda i: (i, 0)), jax.ShapeDtypeStruct((t_rows, width), dtype)

    outs = [out(A_WIDTH, BF16), out(KV_WIDTH, BF16), out(KV_WIDTH, BF16),
            out(C_WIDTH, BF16), out(KV_WIDTH, BF16), out(KV_WIDTH, BF16), out(B_WIDTH, F32)]
    return pl.pallas_call(
        functools.partial(_inproj_kernel, tm=tm, n_lat_tiles=n_lat_tiles, split_input=split_input,
                          sub=min(256, tm)),
        grid=(t_rows // tm,),
        in_specs=x_specs + [
            mod_spec(0), mod_spec(1),
            full((1, d)),
            full((d, IN_WIDTH)),
            rope_spec(), rope_spec(), rope_spec(),
            full((1, HEAD_DIM)), full((1, HEAD_DIM)), full((1, HEAD_DIM)), full((1, HEAD_DIM)),
            full((1, B_WIDTH)),
            full((B_GROUPS, B_CHUNK, B_CHUNK)),
            full((B_GROUPS, B_CHUNK, HEAD_DIM)),
        ],
        out_specs=[o[0] for o in outs],
        out_shape=[o[1] for o in outs],
        compiler_params=_params(("arbitrary",)),
        name="inproj",
    )(*x_args, modt, modt, norm_g, w_in_bf, cos_t, s1_t, s2_t, qn_a, kn_a, qn_c, kn_c, vn_b, ws_bf, bsb)


def _softmax_pv(s_list, v_list, sink):
    m = s_list[0].max(axis=-1, keepdims=True)
    for s in s_list[1:]:
        m = jnp.maximum(m, s.max(axis=-1, keepdims=True))
    if sink is not None:
        m = jnp.maximum(m, sink)
    l = None
    o = None
    for s, v in zip(s_list, v_list):
        p = jnp.exp2(s - m)
        ls = jnp.sum(p, axis=-1, keepdims=True)
        os_ = jnp.dot(p.astype(BF16), v, preferred_element_type=F32)
        l = ls if l is None else l + ls
        o = os_ if o is None else o + os_
    if sink is not None:
        l = l + jnp.exp2(sink - m)
    return o / l


def _attn_win_kernel(sink_ref, q_ref, kx_ref, kp_ref, km_ref, kn_ref, vx_ref, vp_ref, vm_ref, vn_ref,
                     o_ref, s_ref, *, tq, seq, n_g):
    i = pl.program_id(1)
    kvh = pl.program_id(2)
    kband = jnp.concatenate([kp_ref[...], km_ref[...], kn_ref[...]], axis=0)
    vband = jnp.concatenate([vp_ref[...], vm_ref[...], vn_ref[...]], axis=0)
    kx, vx = kx_ref[...], vx_ref[...]
    n_ctx = kx.shape[0]
    n_slots = s_ref.shape[0]
    nb = 3 * WINDOW
    rows_all = n_g * WINDOW
    r = lax.broadcasted_iota(I32, (rows_all, nb), 0) % WINDOW
    c = lax.broadcasted_iota(I32, (rows_all, nb), 1)
    head = lax.broadcasted_iota(I32, (rows_all, 1), 0) // WINDOW
    sink = sink_ref[kvh * n_g] * LOG2E
    for g in range(1, n_g):
        sink = jnp.where(head == g, sink_ref[kvh * n_g + g] * LOG2E, sink)

    def scores(j):
        rows = slice(j * WINDOW, (j + 1) * WINDOW)
        q = jnp.concatenate([q_ref[rows, g * HEAD_DIM:(g + 1) * HEAD_DIM] for g in range(n_g)], axis=0)
        kpos = c + (i * tq + (j - 1) * WINDOW)
        valid = (jnp.abs(c - WINDOW - r) <= WINDOW) & (kpos >= 0) & (kpos < seq)
        s_ref[j % n_slots, :, :n_ctx] = _dot_nt(q, kx)
        s_ref[j % n_slots, :, n_ctx:] = jnp.where(valid, _dot_nt(q, kband[j * WINDOW:j * WINDOW + nb, :]), NEG_INF)

    n_blocks = tq // WINDOW
    for j in range(min(PIPELINE_DEPTH, n_blocks)):
        scores(j)
    for j in range(n_blocks):
        if j + PIPELINE_DEPTH < n_blocks:
            scores(j + PIPELINE_DEPTH)
        slot = j % n_slots
        m = jnp.maximum(s_ref[slot].max(axis=-1, keepdims=True), sink)
        p_x = jnp.exp2(s_ref[slot, :, :n_ctx] - m)
        p_b = jnp.exp2(s_ref[slot, :, n_ctx:] - m)
        l = jnp.sum(p_x, axis=-1, keepdims=True) + jnp.sum(p_b, axis=-1, keepdims=True) + jnp.exp2(sink - m)
        o = (jnp.dot(p_x.astype(BF16), vx, preferred_element_type=F32)
             + jnp.dot(p_b.astype(BF16), vband[j * WINDOW:j * WINDOW + nb, :], preferred_element_type=F32)) / l
        for g in range(n_g):
            o_ref[j * WINDOW:(j + 1) * WINDOW, g * HEAD_DIM:(g + 1) * HEAD_DIM] = o[g * WINDOW:(g + 1) * WINDOW, :]


def _attn_win(sink, qa, ka, va, dims):
    n_batch, seq, ctx_len, _ = dims
    t_rows = n_batch * seq
    tq = _pick_tile(1024, seq)
    nq = seq // tq
    n_g = A_HEADS // A_KV
    wb = tq // WINDOW
    sb = seq // WINDOW
    lat_ctx_blocks = n_batch * seq // ctx_len

    def q_map(b, i, k):
        return (b * nq + i, k)

    def prev_map(b, i, k):
        return (b * sb + jnp.maximum(i * wb - 1, 0), k)

    def next_map(b, i, k):
        return (b * sb + jnp.minimum((i + 1) * wb, sb - 1), k)

    def ctx_map(b, i, k):
        return (lat_ctx_blocks + b, k)

    kv_specs = [pl.BlockSpec((ctx_len, HEAD_DIM), ctx_map), pl.BlockSpec((WINDOW, HEAD_DIM), prev_map),
                pl.BlockSpec((tq, HEAD_DIM), q_map), pl.BlockSpec((WINDOW, HEAD_DIM), next_map)]
    return pl.pallas_call(
        functools.partial(_attn_win_kernel, tq=tq, seq=seq, n_g=n_g),
        grid=(n_batch, nq, A_KV),
        in_specs=[pl.BlockSpec(memory_space=pltpu.SMEM),
                  pl.BlockSpec((tq, n_g * HEAD_DIM), q_map)] + kv_specs + kv_specs,
        out_specs=pl.BlockSpec((tq, n_g * HEAD_DIM), q_map),
        out_shape=jax.ShapeDtypeStruct((t_rows, A_WIDTH), F32),
        scratch_shapes=[pltpu.VMEM((PIPELINE_DEPTH + 1, n_g * WINDOW, ctx_len + 3 * WINDOW), F32)],
        compiler_params=_params(("arbitrary",) * 3),
        name="attn_window",
    )(sink, qa, ka, ka, ka, ka, va, va, va, va)


def _attn_glob_kernel(q_ref, kx_ref, kl_ref, vx_ref, vl_ref, o_ref, s_ref, *, n_g, sub):
    kx, kl, vx, vl = kx_ref[...], kl_ref[...], vx_ref[...], vl_ref[...]
    n_ctx = kx.shape[0]
    chains = [(slice(r0, r0 + sub), slice(g * HEAD_DIM, (g + 1) * HEAD_DIM))
              for r0 in range(0, q_ref.shape[0], sub) for g in range(n_g)]
    n_slots = s_ref.shape[0]

    def scores(i):
        q = q_ref[chains[i][0], chains[i][1]]
        s_ref[i % n_slots, :, :n_ctx] = _dot_nt(q, kx)
        s_ref[i % n_slots, :, n_ctx:] = _dot_nt(q, kl)

    for i in range(min(PIPELINE_DEPTH, len(chains))):
        scores(i)
    for i, chain in enumerate(chains):
        if i + PIPELINE_DEPTH < len(chains):
            scores(i + PIPELINE_DEPTH)
        slot = i % n_slots
        m = s_ref[slot].max(axis=-1, keepdims=True)
        p_x = jnp.exp2(s_ref[slot, :, :n_ctx] - m)
        p_l = jnp.exp2(s_ref[slot, :, n_ctx:] - m)
        l = jnp.sum(p_x, axis=-1, keepdims=True) + jnp.sum(p_l, axis=-1, keepdims=True)
        o = (jnp.dot(p_x.astype(BF16), vx, preferred_element_type=F32)
             + jnp.dot(p_l.astype(BF16), vl, preferred_element_type=F32))
        o_ref[chain[0], chain[1]] = o / l


def _attn_glob(qc, kc, vc, dims):
    n_batch, seq, ctx_len, _ = dims
    t_rows = n_batch * seq
    tq = _pick_tile(512, seq)
    sub = min(128, tq)
    nq = seq // tq
    n_g = C_HEADS // C_KV
    lat_ctx_blocks = n_batch * seq // ctx_len

    def q_map(b, k, i):
        return (b * nq + i, k)

    kv_specs = [pl.BlockSpec((ctx_len, HEAD_DIM), lambda b, k, i: (lat_ctx_blocks + b, k)),
                pl.BlockSpec((seq, HEAD_DIM), lambda b, k, i: (b, k))]
    return pl.pallas_call(
        functools.partial(_attn_glob_kernel, n_g=n_g, sub=sub),
        grid=(n_batch, C_KV, nq),
        in_specs=[pl.BlockSpec((tq, n_g * HEAD_DIM), q_map)] + kv_specs + kv_specs,
        out_specs=pl.BlockSpec((tq, n_g * HEAD_DIM), q_map),
        out_shape=jax.ShapeDtypeStruct((t_rows, C_WIDTH), F32),
        scratch_shapes=[pltpu.VMEM((PIPELINE_DEPTH + 1, sub, ctx_len + seq), F32)],
        compiler_params=_params(("arbitrary",) * 3),
        name="attn_global",
    )(qc, kc, kc, vc, vc)


def _attn_ctx_kernel(sink_ref, q_ref, k_ref, v_ref, o_ref, *, n_g, use_sink):
    kvh = pl.program_id(1)
    k, v = k_ref[...], v_ref[...]
    for g in range(n_g):
        sl = slice(g * HEAD_DIM, (g + 1) * HEAD_DIM)
        sink = sink_ref[kvh * n_g + g] * LOG2E if use_sink else None
        o_ref[:, sl] = _softmax_pv([_dot_nt(q_ref[:, sl], k)], [v], sink)


def _attn_ctx(sink, q, k, v, n_heads, n_kv, use_sink, dims):
    n_batch, seq, ctx_len, _ = dims
    n_g = n_heads // n_kv
    lat_ctx_blocks = n_batch * seq // ctx_len

    def blk(b, k):
        return (lat_ctx_blocks + b, k)

    return pl.pallas_call(
        functools.partial(_attn_ctx_kernel, n_g=n_g, use_sink=use_sink),
        grid=(n_batch, n_kv),
        in_specs=[pl.BlockSpec(memory_space=pltpu.SMEM),
                  pl.BlockSpec((ctx_len, n_g * HEAD_DIM), blk),
                  pl.BlockSpec((ctx_len, HEAD_DIM), blk),
                  pl.BlockSpec((ctx_len, HEAD_DIM), blk)],
        out_specs=pl.BlockSpec((ctx_len, n_g * HEAD_DIM), lambda b, k: (b, k)),
        out_shape=jax.ShapeDtypeStruct((n_batch * ctx_len, n_heads * HEAD_DIM), F32),
        compiler_params=_params(("arbitrary",) * 2),
        name="attn_context",
    )(sink, q, k, v)


def _merge_kernel(*refs, n_lat_tiles, has_ctx, split_x, sub):
    if has_ctx:
        oa_ref, oc_ref, oax_ref, ocx_ref = refs[:4]
        refs = refs[4:]
    else:
        oa_ref, oc_ref = refs[:2]
        refs = refs[2:]
    if split_x:
        x_ref, xc_ref = refs[:2]
        refs = refs[2:]
    else:
        x_ref = refs[0]
        refs = refs[1:]
    ob_ref, og_ref, w_ref, g1_ref, sh_ref, sc_ref, n2_ref, wr_ref, xo_ref, h_ref, aff_ref = refs
    is_lat = pl.program_id(0) < n_lat_tiles
    n_exp = aff_ref.shape[1]
    half = x_ref.shape[1] // 2

    def group(o, off, width):
        y = (_rms_scale(o) * og_ref[:, off:off + width]).astype(BF16)
        return jnp.dot(y, w_ref[off:off + width, :], preferred_element_type=F32)

    for r0 in range(0, x_ref.shape[0], sub):
        rows = slice(r0, r0 + sub)
        o_a, o_c = oa_ref[rows, :], oc_ref[rows, :]
        if has_ctx:
            o_a = jnp.where(is_lat, o_a, oax_ref[rows, :])
            o_c = jnp.where(is_lat, o_c, ocx_ref[rows, :])
        y = (group(o_a, 0, A_WIDTH) + group(ob_ref[rows, :], A_WIDTH, B_WIDTH)
             + group(o_c, A_WIDTH + B_WIDTH, C_WIDTH))
        x = x_ref[rows, :]
        if split_x:
            x = jnp.where(is_lat, x, xc_ref[rows, :])
        xn = x + g1_ref[0] * y
        xo_ref[rows, :] = xn
        h = _rms_scale(xn) * n2_ref[...]
        h = h * (1.0 + sc_ref[0]) + sh_ref[0]
        hb = h.astype(BF16)
        lo = lax.bitcast_convert_type(hb[:, :half].astype(F32), U32) >> 16
        hi = lax.bitcast_convert_type(hb[:, half:].astype(F32), U32) & jnp.uint32(0xFFFF0000)
        h_ref[rows, :] = lo | hi
        h_lo = (h - hb.astype(F32)).astype(BF16)
        both = jnp.dot(hb, wr_ref[...], preferred_element_type=F32)
        logits = (both[:, :n_exp] + both[:, n_exp:]
                  + jnp.dot(h_lo, wr_ref[:, :n_exp], preferred_element_type=F32))
        e = jnp.exp(logits - logits.max(axis=-1, keepdims=True))
        aff_ref[rows, :] = e / jnp.sum(e, axis=-1, keepdims=True)


def _merge(o_a, o_c, o_ax, o_cx, o_b, out_g, w_out_bf, xs, modt, norm2_g, w_router, rows_out, dims):
    n_batch, seq, ctx_len, d = dims
    tm = _pick_tile(512, seq, n_batch * ctx_len)
    n_lat_tiles = n_batch * seq // tm
    tps = seq // tm
    n_exp = w_router.shape[1]
    mix = A_WIDTH + B_WIDTH + C_WIDTH
    has_ctx = o_ax is not None
    assert has_ctx == (rows_out > n_batch * seq)
    split_x = isinstance(xs, tuple)
    assert has_ctx or not split_x
    w_hi = w_router.astype(BF16)
    w_lo = (w_router - w_hi.astype(F32)).astype(BF16)

    def mod_spec(k):
        return pl.BlockSpec((1, 1, d), lambda i: (_row_seg(i, n_lat_tiles, tps, n_batch) * N_MOD + k, 0, 0))

    def rows(width):
        return pl.BlockSpec((tm, width), lambda i: (i, 0))

    def lat_rows(width):
        return pl.BlockSpec((tm, width), lambda i: (jnp.minimum(i, n_lat_tiles - 1), 0))

    def ctx_rows(width):
        return pl.BlockSpec((tm, width), lambda i: (jnp.maximum(i - n_lat_tiles, 0), 0))

    def full(shape):
        return pl.BlockSpec(shape, lambda i: (0,) * len(shape))

    attn_specs = [lat_rows(A_WIDTH), lat_rows(C_WIDTH)]
    attn_args = [o_a, o_c]
    if has_ctx:
        attn_specs += [ctx_rows(A_WIDTH), ctx_rows(C_WIDTH)]
        attn_args += [o_ax, o_cx]
    if split_x:
        attn_specs += [lat_rows(d), ctx_rows(d)]
        attn_args += list(xs)
    else:
        attn_specs += [rows(d)]
        attn_args += [xs]
    return pl.pallas_call(
        functools.partial(_merge_kernel, n_lat_tiles=n_lat_tiles, has_ctx=has_ctx, split_x=split_x,
                          sub=min(256, tm)),
        grid=(rows_out // tm,),
        in_specs=attn_specs + [rows(B_WIDTH), full((1, mix)), full((mix, d)),
                               mod_spec(2), mod_spec(3), mod_spec(4), full((1, d)), full((d, 2 * n_exp))],
        out_specs=[rows(d), rows(d // 2), rows(n_exp)],
        out_shape=[jax.ShapeDtypeStruct((rows_out, d), F32), jax.ShapeDtypeStruct((rows_out, d // 2), U32),
                   jax.ShapeDtypeStruct((rows_out, n_exp), F32)],
        compiler_params=_params(("arbitrary",)),
        name="merge_router",
    )(*attn_args, o_b, out_g, w_out_bf, modt, modt, modt, norm2_g, jnp.concatenate([w_hi, w_lo], axis=1))


def _prefix_excl(mask_f32, tri):
    rows, n = mask_f32.shape
    cw = tri.shape[0]
    carry = jnp.zeros((rows, 1), F32)
    parts = []
    for c in range(n // cw):
        m = mask_f32[:, c * cw:(c + 1) * cw]
        parts.append(jnp.dot(m.astype(BF16), tri, preferred_element_type=F32) + carry)
        carry = carry + jnp.sum(m, axis=-1, keepdims=True)
    return parts[0] if len(parts) == 1 else jnp.concatenate(parts, axis=-1)


def _route_kernel(aff_ref, idx_ref, gate_ref, *, n, cap, n_exp, cb):
    a = aff_ref[0]
    bits = pltpu.bitcast(a, I32)

    def bit_step(b, thr):
        cand = thr | jnp.left_shift(jnp.int32(1), 30 - b)
        cnt = jnp.sum(jnp.where(bits >= cand, 1.0, 0.0), axis=-1, keepdims=True)
        return jnp.where(cnt >= cap, cand, thr)

    thr = lax.fori_loop(0, 31, bit_step, jnp.zeros((n_exp, 1), I32))
    gt = bits > thr
    eq = bits == thr
    cw = min(256, n)
    tri = jnp.where(lax.broadcasted_iota(I32, (cw, cw), 0) < lax.broadcasted_iota(I32, (cw, cw), 1),
                    1.0, 0.0).astype(BF16)
    need = cap - jnp.sum(jnp.where(gt, 1.0, 0.0), axis=-1, keepdims=True)
    sel = gt | (eq & (_prefix_excl(jnp.where(eq, 1.0, 0.0), tri) < need))
    pos = jnp.where(sel, _prefix_excl(jnp.where(sel, 1.0, 0.0), tri), -1.0)

    tok = lax.broadcasted_iota(I32, (1, n), 1)
    t_hi = (tok // 64).astype(F32)
    t_lo = (tok % 64).astype(F32)
    g_hi = a.astype(BF16).astype(F32)
    r1 = a - g_hi
    g_mid = r1.astype(BF16).astype(F32)
    g_lo = r1 - g_mid
    ca = cap // cb
    pa = jnp.floor(pos * (1.0 / cb))
    pb = pos - pa * cb
    n_val = 5
    rows_l = -(-n_val * ca // 16) * 16
    rowi = lax.broadcasted_iota(I32, (rows_l, n), 0)
    which = rowi // ca
    pa_row = (rowi % ca).astype(F32)
    pb_row = lax.broadcasted_iota(I32, (cb, n), 0).astype(F32)
    for e in range(n_exp):
        one = slice(e, e + 1)
        val = jnp.where(which == 0, t_hi,
              jnp.where(which == 1, t_lo,
              jnp.where(which == 2, g_hi[one, :],
              jnp.where(which == 3, g_mid[one, :],
              jnp.where(which == 4, g_lo[one, :], 0.0)))))
        lhs = jnp.where(pa[one, :] == pa_row, val, 0.0).astype(BF16)
        rhs = jnp.where(pb[one, :] == pb_row, 1.0, 0.0).astype(BF16)
        res = _dot_nt(lhs, rhs)
        part = lambda k: res[k * ca:(k + 1) * ca, :]
        idx_ref[0, e * ca:(e + 1) * ca, :] = (part(0) * 64.0 + part(1)).astype(I32)
        gate_ref[0, e * ca:(e + 1) * ca, :] = part(2) + part(3) + part(4)


def _route(aff_sets):
    n_sets, n_exp, n = aff_sets.shape
    cap = EC_CAPACITY * n // n_exp
    cb = min(32, cap)
    assert cap % cb == 0 and cb & (cb - 1) == 0
    spec = pl.BlockSpec((1, n_exp * cap // cb, cb), lambda s: (s, 0, 0))
    idx, gate = pl.pallas_call(
        functools.partial(_route_kernel, n=n, cap=cap, n_exp=n_exp, cb=cb),
        grid=(n_sets,),
        in_specs=[pl.BlockSpec((1, n_exp, n), lambda s: (s, 0, 0))],
        out_specs=[spec, spec],
        out_shape=[jax.ShapeDtypeStruct((n_sets, n_exp * cap // cb, cb), I32),
                   jax.ShapeDtypeStruct((n_sets, n_exp * cap // cb, cb), F32)],
        compiler_params=_params(("arbitrary",)),
        name="route",
    )(aff_sets)
    return idx.reshape(n_sets, n_exp, cap), gate.reshape(n_sets, n_exp, cap)


def _ffn_kernel(idx_ref, hpk_hbm, gate_ref, g2_ref, wg_ref, wu_ref, wd_ref, x_in_hbm, x_hbm,
                hpk, hb, yacc, xbuf, sem_h, sem_x, sem_s, *, m, n_f, n_groups, groups, n_split, per, ctx_seg):
    del x_in_hbm
    e = pl.program_id(0)
    part = pl.program_id(1)
    f = pl.program_id(2)
    g = e * n_split + part
    slot = g % 2
    g_prev = jnp.maximum(g - 1, 0)
    g_next = (g + 1) % n_groups
    half = n_f // 2
    rows_rw = m // half
    h_steps = max(n_f - 1, 1)
    h_base = -(-m // (h_steps * SUBLANES)) * SUBLANES
    h_parts = [(min(k * h_base, m), min((k + 1) * h_base, m)) for k in range(h_steps)]
    x_steps = list(range(half, n_f - 1)) or [n_f - 1]
    fused_tail = n_f - 1 not in x_steps
    x_base = -(-m // (len(x_steps) * SUBLANES)) * SUBLANES
    x_parts = {st: (min(k * x_base, m), min((k + 1) * x_base, m)) for k, st in enumerate(x_steps)}

    def h_copy(r, j, dst_slot):
        return pltpu.make_async_copy(hpk_hbm.at[pl.ds(r, 1), :], hpk.at[dst_slot, pl.ds(j, 1), :],
                                     sem_h.at[dst_slot])

    def x_copy(r, j):
        return pltpu.make_async_copy(x_hbm.at[pl.ds(r, 1), :], xbuf.at[pl.ds(j, 1), :], sem_x)

    def s_copy(r, j):
        return pltpu.make_async_copy(xbuf.at[pl.ds(j, 1), :], x_hbm.at[pl.ds(r, 1), :], sem_s)

    def start_rows(copy, grp, first, n):
        for j in range(n):
            copy(idx_ref[grp * m + first + j], first + j).start()

    def wait_h(s):
        pltpu.make_async_copy(hpk.at[1 - s], hpk.at[s], sem_h.at[s]).wait()

    def wait_rows(sem):
        pltpu.make_async_copy(yacc, xbuf, sem).wait()

    def for_rows(fn):
        def body(j, carry):
            fn(j)
            return carry
        lax.fori_loop(0, m, body, 0)

    @pl.when((g == 0) & (f == 0))
    def _():
        for_rows(lambda j: h_copy(idx_ref[j], j, 0).start())
        for_rows(lambda j: x_copy(idx_ref[j], j).start())
        wait_rows(sem_x)

    @pl.when(f == 0)
    def _():
        wait_h(slot)
        p = hpk[slot]
        d_half = hpk.shape[2]
        hb[:, :d_half] = lax.bitcast_convert_type(p << 16, F32).astype(BF16)
        hb[:, d_half:] = lax.bitcast_convert_type(p & jnp.uint32(0xFFFF0000), F32).astype(BF16)

    @pl.when(f == half)
    def _():
        wait_rows(sem_s)

    def accumulate_into_rows(total):
        for off, size, sample in groups:
            seg = ctx_seg if sample is None else part * per + sample
            rows = pl.ds(off, size)
            xbuf[rows, :] = xbuf[rows, :] + g2_ref[pl.ds(seg, 1), :] * (gate_ref[0, 0, rows, :]
                                                                      * total[off:off + size, :])

    def matmuls(first, last):
        h = hb[...]
        a = jnp.dot(h, wg_ref[...].astype(BF16), preferred_element_type=F32)
        u = jnp.dot(h, wu_ref[...].astype(BF16), preferred_element_type=F32)
        z = (a / (1.0 + jnp.exp(-a)) * u).astype(BF16)
        y = jnp.dot(z, wd_ref[...].astype(BF16), preferred_element_type=F32)
        if last:
            accumulate_into_rows(yacc[...] + y)
        elif first:
            yacc[...] = y
        else:
            yacc[...] += y

    for step in range(n_f):
        @pl.when(f == step)
        def _(step=step):
            if step < half:
                start_rows(s_copy, g_prev, step * rows_rw, rows_rw)
            if step in x_parts:
                start_rows(x_copy, g, x_parts[step][0], x_parts[step][1] - x_parts[step][0])
            if fused_tail and step == n_f - 1:
                wait_rows(sem_x)
            if step < h_steps:
                start_rows(lambda r, j: h_copy(r, j, 1 - slot), g_next, h_parts[step][0],
                           h_parts[step][1] - h_parts[step][0])
            matmuls(first=step == 0, last=fused_tail and step == n_f - 1)

    if not fused_tail:
        @pl.when(f == n_f - 1)
        def _():
            wait_rows(sem_x)
            accumulate_into_rows(yacc[...])

    @pl.when((g == n_groups - 1) & (f == n_f - 1))
    def _():
        for_rows(lambda j: s_copy(idx_ref[g * m + j], j).start())
        wait_rows(sem_s)
        wait_h(1 - slot)


def _experts(idx_flat, gate, hpk, g2tab, w_gate, w_up, w_down, xs, layer, m, groups, n_split, per, ctx_seg):
    d = xs.shape[1]
    n_exp, d_exp = w_gate.shape[1], w_gate.shape[3]
    tf = _pick_tile(256, d_exp // 2)
    n_f = d_exp // tf
    n_groups = n_exp * n_split
    assert n_f % 2 == 0 and m % (n_f // 2) == 0 and n_groups % 2 == 0
    grid_spec = pltpu.PrefetchScalarGridSpec(
        num_scalar_prefetch=1,
        grid=(n_exp, n_split, n_f),
        in_specs=[
            pl.BlockSpec(memory_space=pl.ANY),
            pl.BlockSpec((1, 1, m, 1), lambda e, p, f, idx: (e, p, 0, 0)),
            pl.BlockSpec(g2tab.shape, lambda e, p, f, idx: (0, 0)),
            pl.BlockSpec((None, None, d, tf), lambda e, p, f, idx: (layer, e, 0, f)),
            pl.BlockSpec((None, None, d, tf), lambda e, p, f, idx: (layer, e, 0, f)),
            pl.BlockSpec((None, None, tf, d), lambda e, p, f, idx: (layer, e, f, 0)),
            pl.BlockSpec(memory_space=pl.ANY),
        ],
        out_specs=pl.BlockSpec(memory_space=pl.ANY),
        scratch_shapes=[pltpu.VMEM((2, m, d // 2), U32), pltpu.VMEM((m, d), BF16), pltpu.VMEM((m, d), F32),
                        pltpu.VMEM((m, d), F32), pltpu.SemaphoreType.DMA((2,)), pltpu.SemaphoreType.DMA,
                        pltpu.SemaphoreType.DMA],
    )
    return pl.pallas_call(
        functools.partial(_ffn_kernel, m=m, n_f=n_f, n_groups=n_groups, groups=tuple(groups), n_split=n_split,
                          per=per, ctx_seg=ctx_seg),
        grid_spec=grid_spec,
        out_shape=jax.ShapeDtypeStruct(xs.shape, xs.dtype),
        input_output_aliases={7: 0},
        compiler_params=_params(("arbitrary",) * 3),
        name="experts",
    )(idx_flat, hpk, gate, g2tab, w_gate, w_up, w_down, xs)


def _rope_tables(seq):
    pos = np.arange(seq)
    n_freq = HEAD_DIM // 4
    inv = jnp.asarray(ROPE_THETA, F32) ** (-jnp.arange(n_freq, dtype=F32) / n_freq)
    row = jnp.asarray(pos // GRID_W, F32)[:, None] * inv
    col = jnp.asarray(pos % GRID_W, F32)[:, None] * inv
    zero = jnp.zeros_like(row)
    cos = jnp.concatenate([jnp.cos(row), jnp.cos(row), jnp.cos(col), jnp.cos(col)], axis=-1)
    s1 = jnp.concatenate([zero, jnp.sin(row), zero, jnp.sin(col)], axis=-1)
    s2 = jnp.concatenate([-jnp.sin(row), zero, -jnp.sin(col), zero], axis=-1)

    def padded(tm):
        ident = jnp.ones((tm, HEAD_DIM), F32)
        none = jnp.zeros((tm, HEAD_DIM), F32)
        return (jnp.concatenate([cos, ident], axis=0), jnp.concatenate([s1, none], axis=0),
                jnp.concatenate([s2, none], axis=0))

    return padded


def _expert_rows(idx_l, gate_l, idx_c, gate_c, dims, n_split):
    n_batch, seq, ctx_len, _ = dims
    per = n_batch // n_split
    n_exp = idx_l.shape[1]

    def arrange(v):
        s = v.shape[-1]
        return v.reshape(n_split, per, n_exp, s).transpose(2, 0, 1, 3).reshape(n_exp, n_split, per * s)

    rows_l = idx_l + (jnp.arange(n_batch, dtype=I32) * seq)[:, None, None]
    parts_i, parts_g = [arrange(rows_l)], [arrange(gate_l)]
    groups = [(b * idx_l.shape[-1], idx_l.shape[-1], b) for b in range(per)]
    if idx_c is not None:
        rows_c = idx_c + (n_batch * seq + jnp.arange(n_batch, dtype=I32) * ctx_len)[:, None, None]
        parts_i.append(arrange(rows_c))
        parts_g.append(arrange(gate_c))
        groups += [(per * idx_l.shape[-1], per * idx_c.shape[-1], None)]
    idx = jnp.concatenate(parts_i, axis=-1)
    gate = jnp.concatenate(parts_g, axis=-1)
    return idx, gate, groups


def kernel(x, c, ctx, c_ctx, w_mod, b_mod, norm1_g, norm2_g, w_in, qn_a, kn_a, sink_a, vn_b, w_s, b_s,
           qn_c, kn_c, out_g, w_out, w_router, w_gate, w_up, w_down):
    n_batch, seq, d = x.shape
    ctx_len = ctx.shape[1]
    depth = w_mod.shape[0]
    n_exp = w_router.shape[2]
    dims = (n_batch, seq, ctx_len, d)
    t_lat = n_batch * seq
    assert n_batch < MOD_ROWS and seq % WINDOW == 0 and ctx_len % B_CHUNK == 0 and t_lat % ctx_len == 0
    n_split = 2 if n_batch % 2 == 0 else 1
    per = n_batch // n_split

    xs = (x.reshape(t_lat, d), ctx.reshape(n_batch * ctx_len, d))
    cvec = jnp.zeros((MOD_ROWS, d), F32).at[:n_batch].set(c).at[n_batch].set(c_ctx)
    rope = _rope_tables(seq)

    for layer in range(depth):
        last = layer == depth - 1
        mod = _modulation(cvec, w_mod, b_mod, layer)
        modt = mod.reshape(MOD_ROWS * N_MOD, 1, d)
        g2tab = mod.reshape(MOD_ROWS, N_MOD, d)[:, N_MOD - 1, :]
        row = lambda v: v[layer].reshape(1, -1)
        bsb = jnp.broadcast_to(b_s[layer][:, :, None], (B_GROUPS, B_CHUNK, HEAD_DIM))
        qa, ka, va, qc, kc, vc, o_b = _inproj(
            xs, modt, row(norm1_g), w_in[layer].astype(BF16), rope, row(qn_a), row(kn_a), row(qn_c),
            row(kn_c), row(vn_b), w_s[layer].astype(BF16), bsb, dims)
        o_a = _attn_win(sink_a[layer], qa, ka, va, dims)
        o_c = _attn_glob(qc, kc, vc, dims)
        o_ax = o_cx = None
        if not last:
            o_ax = _attn_ctx(sink_a[layer], qa, ka, va, A_HEADS, A_KV, True, dims)
            o_cx = _attn_ctx(sink_a[layer], qc, kc, vc, C_HEADS, C_KV, False, dims)
        rows_out = t_lat if last else t_lat + n_batch * ctx_len
        xs, h2, aff = _merge(o_a, o_c, o_ax, o_cx, o_b, row(out_g), w_out[layer].astype(BF16), xs, modt,
                             row(norm2_g), w_router[layer], rows_out, dims)

        idx_l, gate_l = _route(aff[:t_lat].reshape(n_batch, seq, n_exp).transpose(0, 2, 1))
        idx_c = gate_c = None
        if not last:
            idx_c, gate_c = _route(aff[t_lat:].reshape(n_batch, ctx_len, n_exp).transpose(0, 2, 1))
        idx, gate, groups = _expert_rows(idx_l, gate_l, idx_c, gate_c, dims, n_split)
        m = idx.shape[-1]
        xs = _experts(idx.reshape(-1), gate.reshape(n_exp, n_split, m, 1), h2, g2tab, w_gate, w_up, w_down,
                      xs, layer, m, groups, n_split, per, n_batch)
    return xs.reshape(n_batch, seq, d)
```

```python
import functools

import jax
import jax.numpy as jnp
import numpy as np
from jax import lax
from jax.experimental import pallas as pl
from jax.experimental.pallas import tpu as pltpu

F32 = jnp.float32
BF16 = jnp.bfloat16
I32 = jnp.int32
U32 = jnp.uint32

HEAD_DIM = 128
GRID_W = 64
WINDOW = 128
A_HEADS, A_KV = 6, 2
B_GROUPS, B_CHUNK = 4, 128
C_HEADS, C_KV = 6, 2
A_WIDTH = A_HEADS * HEAD_DIM
B_WIDTH = B_GROUPS * HEAD_DIM
C_WIDTH = C_HEADS * HEAD_DIM
KV_WIDTH = A_KV * HEAD_DIM
OFF_QA = 0
OFF_KA = OFF_QA + A_WIDTH
OFF_VA = OFF_KA + KV_WIDTH
OFF_UB = OFF_VA + KV_WIDTH
OFF_VB = OFF_UB + B_WIDTH
OFF_QC = OFF_VB + B_WIDTH
OFF_KC = OFF_QC + C_WIDTH
OFF_VC = OFF_KC + KV_WIDTH
IN_WIDTH = OFF_VC + KV_WIDTH
EC_CAPACITY = 2
N_MOD = 6
ROPE_THETA = 10000.0
EPS = 1e-6
NEG_INF = -1e30
LOG2E = 1.4426950408889634
QK_SCALE = HEAD_DIM ** -0.5 * LOG2E
SUBLANES = 8
PIPELINE_DEPTH = 2
MOD_ROWS = 8

VMEM_LIMIT = 56 * 1024 * 1024

def _params(sem, vmem=VMEM_LIMIT):
    return pltpu.CompilerParams(dimension_semantics=sem, vmem_limit_bytes=vmem)


def _pick_tile(target, *sizes):
    t = target
    while any(s % t for s in sizes):
        t //= 2
    return t


def _rms_scale(x):
    return x * lax.rsqrt(jnp.mean(x * x, axis=-1, keepdims=True) + EPS)


def _dot_nt(a, b):
    return lax.dot_general(a, b, (((1,), (1,)), ((), ())), preferred_element_type=F32)


def _mod_kernel(c_ref, w_ref, b_ref, o_ref):
    c = c_ref[...]
    s = c / (1.0 + jnp.exp(-c))
    o_ref[...] = jnp.dot(s, w_ref[...], precision=lax.Precision.HIGHEST,
                         preferred_element_type=F32) + b_ref[...]


def _modulation(cvec, w_mod, b_mod, layer):
    d = cvec.shape[1]
    n = w_mod.shape[2]
    tn = _pick_tile(1024, n)
    return pl.pallas_call(
        _mod_kernel,
        grid=(n // tn,),
        in_specs=[
            pl.BlockSpec((MOD_ROWS, d), lambda j: (0, 0)),
            pl.BlockSpec((None, d, tn), lambda j: (layer, 0, j)),
            pl.BlockSpec((None, 1, tn), lambda j: (layer, 0, j)),
        ],
        out_specs=pl.BlockSpec((MOD_ROWS, tn), lambda j: (0, j)),
        out_shape=jax.ShapeDtypeStruct((MOD_ROWS, n), F32),
        compiler_params=_params(("arbitrary",)),
        name="modulation",
    )(cvec, w_mod, b_mod.reshape(b_mod.shape[0], 1, n))


def _inproj_kernel(*refs, tm, n_lat_tiles, split_input, sub):
    if split_input:
        xl_ref, xc_ref = refs[:2]
        refs = refs[2:]
    else:
        xl_ref = refs[0]
        refs = refs[1:]
    (sh_ref, sc_ref, g_ref, w_ref, cos_ref, s1_ref, s2_ref, qna_ref, kna_ref, qnc_ref, knc_ref, vnb_ref, ws_ref,
     bsb_ref, qa_ref, ka_ref, va_ref, qc_ref, kc_ref, vc_ref, ob_ref) = refs

    for r0 in range(0, tm, sub):
        rows = slice(r0, r0 + sub)
        x = xl_ref[rows, :]
        if split_input:
            x = jnp.where(pl.program_id(0) < n_lat_tiles, x, xc_ref[rows, :])
        h = _rms_scale(x) * g_ref[...]
        h = h * (1.0 + sc_ref[0]) + sh_ref[0]
        hb = h.astype(BF16)
        cos, s1, s2 = cos_ref[rows, :], s1_ref[rows, :], s2_ref[rows, :]

        def proj(off, width):
            return jnp.dot(hb, w_ref[:, off:off + width], preferred_element_type=F32)

        def qk_head(p, gain, scale):
            q = _rms_scale(p) * gain
            q = q * cos + pltpu.roll(q, 32, 1) * s1 + pltpu.roll(q, 96, 1) * s2
            if scale != 1.0:
                q = q * scale
            return q.astype(BF16)

        def heads(off, n_heads, gain_ref, scale, out_ref):
            p = proj(off, n_heads * HEAD_DIM)
            for hd in range(n_heads):
                sl = slice(hd * HEAD_DIM, (hd + 1) * HEAD_DIM)
                out_ref[rows, sl] = qk_head(p[:, sl], gain_ref[...], scale)

        pu = proj(OFF_UB, B_WIDTH)
        pv = proj(OFF_VB, B_WIDTH)
        heads(OFF_QA, A_HEADS, qna_ref, QK_SCALE, qa_ref)
        heads(OFF_KA, A_KV, kna_ref, 1.0, ka_ref)
        va_ref[rows, :] = proj(OFF_VA, KV_WIDTH).astype(BF16)
        heads(OFF_QC, C_HEADS, qnc_ref, QK_SCALE, qc_ref)
        heads(OFF_KC, C_KV, knc_ref, 1.0, kc_ref)
        vc_ref[rows, :] = proj(OFF_VC, KV_WIDTH).astype(BF16)

        for g in range(B_GROUPS):
            sl = slice(g * HEAD_DIM, (g + 1) * HEAD_DIM)
            u = jax.nn.gelu(pu[:, sl])
            vn = (_rms_scale(jax.nn.gelu(pv[:, sl])) * vnb_ref[:, sl]).astype(BF16)
            for c in range(sub // B_CHUNK):
                chunk = slice(c * B_CHUNK, (c + 1) * B_CHUNK)
                mixed = jnp.dot(ws_ref[g], vn[chunk, :], preferred_element_type=F32) + bsb_ref[g]
                ob_ref[r0 + c * B_CHUNK:r0 + (c + 1) * B_CHUNK, sl] = u[chunk, :] * mixed


def _row_seg(i, n_lat_tiles, tiles_per_seq, n_batch):
    return jnp.where(i < n_lat_tiles, i // tiles_per_seq, n_batch)


def _inproj(xs, modt, norm_g, w_in_bf, rope, qn_a, kn_a, qn_c, kn_c, vn_b, ws_bf, bsb, dims):
    n_batch, seq, ctx_len, d = dims
    split_input = isinstance(xs, tuple)
    t_rows = n_batch * (seq + ctx_len)
    tm = _pick_tile(512, seq, n_batch * ctx_len)
    n_lat_tiles = n_batch * seq // tm
    tps = seq // tm
    cos_t, s1_t, s2_t = rope(tm)
    if split_input:
        x_specs = [pl.BlockSpec((tm, d), lambda i: (jnp.minimum(i, n_lat_tiles - 1), 0)),
                   pl.BlockSpec((tm, d), lambda i: (jnp.maximum(i - n_lat_tiles, 0), 0))]
        x_args = list(xs)
    else:
        x_specs = [pl.BlockSpec((tm, d), lambda i: (i, 0))]
        x_args = [xs]

    def mod_spec(k):
        return pl.BlockSpec((1, 1, d), lambda i: (_row_seg(i, n_lat_tiles, tps, n_batch) * N_MOD + k, 0, 0))

    def rope_spec():
        return pl.BlockSpec((tm, HEAD_DIM), lambda i: (jnp.where(i < n_lat_tiles, i % tps, tps), 0))

    def full(shape):
        return pl.BlockSpec(shape, lambda i: (0,) * len(shape))

    def out(width, dtype):
        return pl.BlockSpec((tm, width), lambda i: (i, 0)), jax.ShapeDtypeStruct((t_rows, width), dtype)

    outs = [out(A_WIDTH, BF16), out(KV_WIDTH, BF16), out(KV_WIDTH, BF16),
            out(C_WIDTH, BF16), out(KV_WIDTH, BF16), out(KV_WIDTH, BF16), out(B_WIDTH, F32)]
    return pl.pallas_call(
        functools.partial(_inproj_kernel, tm=tm, n_lat_tiles=n_lat_tiles, split_input=split_input,
                          sub=min(256, tm)),
        grid=(t_rows // tm,),
        in_specs=x_specs + [
            mod_spec(0), mod_spec(1),
            full((1, d)),
            full((d, IN_WIDTH)),
            rope_spec(), rope_spec(), rope_spec(),
            full((1, HEAD_DIM)), full((1, HEAD_DIM)), full((1, HEAD_DIM)), full((1, HEAD_DIM)),
            full((1, B_WIDTH)),
            full((B_GROUPS, B_CHUNK, B_CHUNK)),
            full((B_GROUPS, B_CHUNK, HEAD_DIM)),
        ],
        out_specs=[o[0] for o in outs],
        out_shape=[o[1] for o in outs],
        compiler_params=_params(("arbitrary",)),
        name="inproj",
    )(*x_args, modt, modt, norm_g, w_in_bf, cos_t, s1_t, s2_t, qn_a, kn_a, qn_c, kn_c, vn_b, ws_bf, bsb)


def _softmax_pv(s_list, v_list, sink):
    m = s_list[0].max(axis=-1, keepdims=True)
    for s in s_list[1:]:
        m = jnp.maximum(m, s.max(axis=-1, keepdims=True))
    if sink is not None:
        m = jnp.maximum(m, sink)
    l = None
    o = None
    for s, v in zip(s_list, v_list):
        p = jnp.exp2(s - m)
        ls = jnp.sum(p, axis=-1, keepdims=True)
        os_ = jnp.dot(p.astype(BF16), v, preferred_element_type=F32)
        l = ls if l is None else l + ls
        o = os_ if o is None else o + os_
    if sink is not None:
        l = l + jnp.exp2(sink - m)
    return o / l


def _attn_win_kernel(sink_ref, q_ref, kx_ref, kp_ref, km_ref, kn_ref, vx_ref, vp_ref, vm_ref, vn_ref,
                     o_ref, s_ref, *, tq, seq, n_g):
    i = pl.program_id(1)
    kvh = pl.program_id(2)
    kband = jnp.concatenate([kp_ref[...], km_ref[...], kn_ref[...]], axis=0)
    vband = jnp.concatenate([vp_ref[...], vm_ref[...], vn_ref[...]], axis=0)
    kx, vx = kx_ref[...], vx_ref[...]
    n_ctx = kx.shape[0]
    n_slots = s_ref.shape[0]
    nb = 3 * WINDOW
    rows_all = n_g * WINDOW
    r = lax.broadcasted_iota(I32, (rows_all, nb), 0) % WINDOW
    c = lax.broadcasted_iota(I32, (rows_all, nb), 1)
    head = lax.broadcasted_iota(I32, (rows_all, 1), 0) // WINDOW
    sink = sink_ref[kvh * n_g] * LOG2E
    for g in range(1, n_g):
        sink = jnp.where(head == g, sink_ref[kvh * n_g + g] * LOG2E, sink)

    def scores(j):
        rows = slice(j * WINDOW, (j + 1) * WINDOW)
        q = jnp.concatenate([q_ref[rows, g * HEAD_DIM:(g + 1) * HEAD_DIM] for g in range(n_g)], axis=0)
        kpos = c + (i * tq + (j - 1) * WINDOW)
        valid = (jnp.abs(c - WINDOW - r) <= WINDOW) & (kpos >= 0) & (kpos < seq)
        s_ref[j % n_slots, :, :n_ctx] = _dot_nt(q, kx)
        s_ref[j % n_slots, :, n_ctx:] = jnp.where(valid, _dot_nt(q, kband[j * WINDOW:j * WINDOW + nb, :]), NEG_INF)

    n_blocks = tq // WINDOW
    for j in range(min(PIPELINE_DEPTH, n_blocks)):
        scores(j)
    for j in range(n_blocks):
        if j + PIPELINE_DEPTH < n_blocks:
            scores(j + PIPELINE_DEPTH)
        slot = j % n_slots
        m = jnp.maximum(s_ref[slot].max(axis=-1, keepdims=True), sink)
        p_x = jnp.exp2(s_ref[slot, :, :n_ctx] - m)
        p_b = jnp.exp2(s_ref[slot, :, n_ctx:] - m)
        l = jnp.sum(p_x, axis=-1, keepdims=True) + jnp.sum(p_b, axis=-1, keepdims=True) + jnp.exp2(sink - m)
        o = (jnp.dot(p_x.astype(BF16), vx, preferred_element_type=F32)
             + jnp.dot(p_b.astype(BF16), vband[j * WINDOW:j * WINDOW + nb, :], preferred_element_type=F32)) / l
        for g in range(n_g):
            o_ref[j * WINDOW:(j + 1) * WINDOW, g * HEAD_DIM:(g + 1) * HEAD_DIM] = o[g * WINDOW:(g + 1) * WINDOW, :]


def _attn_win(sink, qa, ka, va, dims):
    n_batch, seq, ctx_len, _ = dims
    t_rows = n_batch * seq
    tq = _pick_tile(1024, seq)
    nq = seq // tq
    n_g = A_HEADS // A_KV
    wb = tq // WINDOW
    sb = seq // WINDOW
    lat_ctx_blocks = n_batch * seq // ctx_len

    def q_map(b, i, k):
        return (b * nq + i, k)

    def prev_map(b, i, k):
        return (b * sb + jnp.maximum(i * wb - 1, 0), k)

    def next_map(b, i, k):
        return (b * sb + jnp.minimum((i + 1) * wb, sb - 1), k)

    def ctx_map(b, i, k):
        return (lat_ctx_blocks + b, k)

    kv_specs = [pl.BlockSpec((ctx_len, HEAD_DIM), ctx_map), pl.BlockSpec((WINDOW, HEAD_DIM), prev_map),
                pl.BlockSpec((tq, HEAD_DIM), q_map), pl.BlockSpec((WINDOW, HEAD_DIM), next_map)]
    return pl.pallas_call(
        functools.partial(_attn_win_kernel, tq=tq, seq=seq, n_g=n_g),
        grid=(n_batch, nq, A_KV),
        in_specs=[pl.BlockSpec(memory_space=pltpu.SMEM),
                  pl.BlockSpec((tq, n_g * HEAD_DIM), q_map)] + kv_specs + kv_specs,
        out_specs=pl.BlockSpec((tq, n_g * HEAD_DIM), q_map),
        out_shape=jax.ShapeDtypeStruct((t_rows, A_WIDTH), F32),
        scratch_shapes=[pltpu.VMEM((PIPELINE_DEPTH + 1, n_g * WINDOW, ctx_len + 3 * WINDOW), F32)],
        compiler_params=_params(("arbitrary",) * 3),
        name="attn_window",
    )(sink, qa, ka, ka, ka, ka, va, va, va, va)


def _attn_glob_kernel(q_ref, kx_ref, kl_ref, vx_ref, vl_ref, o_ref, s_ref, *, n_g, sub):
    kx, kl, vx, vl = kx_ref[...], kl_ref[...], vx_ref[...], vl_ref[...]
    n_ctx = kx.shape[0]
    chains = [(slice(r0, r0 + sub), slice(g * HEAD_DIM, (g + 1) * HEAD_DIM))
              for r0 in range(0, q_ref.shape[0], sub) for g in range(n_g)]
    n_slots = s_ref.shape[0]

    def scores(i):
        q = q_ref[chains[i][0], chains[i][1]]
        s_ref[i % n_slots, :, :n_ctx] = _dot_nt(q, kx)
        s_ref[i % n_slots, :, n_ctx:] = _dot_nt(q, kl)

    for i in range(min(PIPELINE_DEPTH, len(chains))):
        scores(i)
    for i, chain in enumerate(chains):
        if i + PIPELINE_DEPTH < len(chains):
            scores(i + PIPELINE_DEPTH)
        slot = i % n_slots
        m = s_ref[slot].max(axis=-1, keepdims=True)
        p_x = jnp.exp2(s_ref[slot, :, :n_ctx] - m)
        p_l = jnp.exp2(s_ref[slot, :, n_ctx:] - m)
        l = jnp.sum(p_x, axis=-1, keepdims=True) + jnp.sum(p_l, axis=-1, keepdims=True)
        o = (jnp.dot(p_x.astype(BF16), vx, preferred_element_type=F32)
             + jnp.dot(p_l.astype(BF16), vl, preferred_element_type=F32))
        o_ref[chain[0], chain[1]] = o / l


def _attn_glob(qc, kc, vc, dims):
    n_batch, seq, ctx_len, _ = dims
    t_rows = n_batch * seq
    tq = _pick_tile(512, seq)
    sub = min(128, tq)
    nq = seq // tq
    n_g = C_HEADS // C_KV
    lat_ctx_blocks = n_batch * seq // ctx_len

    def q_map(b, k, i):
        return (b * nq + i, k)

    kv_specs = [pl.BlockSpec((ctx_len, HEAD_DIM), lambda b, k, i: (lat_ctx_blocks + b, k)),
                pl.BlockSpec((seq, HEAD_DIM), lambda b, k, i: (b, k))]
    return pl.pallas_call(
        functools.partial(_attn_glob_kernel, n_g=n_g, sub=sub),
        grid=(n_batch, C_KV, nq),
        in_specs=[pl.BlockSpec((tq, n_g * HEAD_DIM), q_map)] + kv_specs + kv_specs,
        out_specs=pl.BlockSpec((tq, n_g * HEAD_DIM), q_map),
        out_shape=jax.ShapeDtypeStruct((t_rows, C_WIDTH), F32),
        scratch_shapes=[pltpu.VMEM((PIPELINE_DEPTH + 1, sub, ctx_len + seq), F32)],
        compiler_params=_params(("arbitrary",) * 3),
        name="attn_global",
    )(qc, kc, kc, vc, vc)


def _attn_ctx_kernel(sink_ref, q_ref, k_ref, v_ref, o_ref, *, n_g, use_sink):
    kvh = pl.program_id(1)
    k, v = k_ref[...], v_ref[...]
    for g in range(n_g):
        sl = slice(g * HEAD_DIM, (g + 1) * HEAD_DIM)
        sink = sink_ref[kvh * n_g + g] * LOG2E if use_sink else None
        o_ref[:, sl] = _softmax_pv([_dot_nt(q_ref[:, sl], k)], [v], sink)


def _attn_ctx(sink, q, k, v, n_heads, n_kv, use_sink, dims):
    n_batch, seq, ctx_len, _ = dims
    n_g = n_heads // n_kv
    lat_ctx_blocks = n_batch * seq // ctx_len

    def blk(b, k):
        return (lat_ctx_blocks + b, k)

    return pl.pallas_call(
        functools.partial(_attn_ctx_kernel, n_g=n_g, use_sink=use_sink),
        grid=(n_batch, n_kv),
        in_specs=[pl.BlockSpec(memory_space=pltpu.SMEM),
                  pl.BlockSpec((ctx_len, n_g * HEAD_DIM), blk),
                  pl.BlockSpec((ctx_len, HEAD_DIM), blk),
                  pl.BlockSpec((ctx_len, HEAD_DIM), blk)],
        out_specs=pl.BlockSpec((ctx_len, n_g * HEAD_DIM), lambda b, k: (b, k)),
        out_shape=jax.ShapeDtypeStruct((n_batch * ctx_len, n_heads * HEAD_DIM), F32),
        compiler_params=_params(("arbitrary",) * 2),
        name="attn_context",
    )(sink, q, k, v)


def _merge_kernel(*refs, n_lat_tiles, has_ctx, split_x, sub):
    if has_ctx:
        oa_ref, oc_ref, oax_ref, ocx_ref = refs[:4]
        refs = refs[4:]
    else:
        oa_ref, oc_ref = refs[:2]
        refs = refs[2:]
    if split_x:
        x_ref, xc_ref = refs[:2]
        refs = refs[2:]
    else:
        x_ref = refs[0]
        refs = refs[1:]
    ob_ref, og_ref, w_ref, g1_ref, sh_ref, sc_ref, n2_ref, wr_ref, xo_ref, h_ref, aff_ref = refs
    is_lat = pl.program_id(0) < n_lat_tiles
    n_exp = aff_ref.shape[1]
    half = x_ref.shape[1] // 2

    def group(o, off, width):
        y = (_rms_scale(o) * og_ref[:, off:off + width]).astype(BF16)
        return jnp.dot(y, w_ref[off:off + width, :], preferred_element_type=F32)

    for r0 in range(0, x_ref.shape[0], sub):
        rows = slice(r0, r0 + sub)
        o_a, o_c = oa_ref[rows, :], oc_ref[rows, :]
        if has_ctx:
            o_a = jnp.where(is_lat, o_a, oax_ref[rows, :])
            o_c = jnp.where(is_lat, o_c, ocx_ref[rows, :])
        y = (group(o_a, 0, A_WIDTH) + group(ob_ref[rows, :], A_WIDTH, B_WIDTH)
             + group(o_c, A_WIDTH + B_WIDTH, C_WIDTH))
        x = x_ref[rows, :]
        if split_x:
            x = jnp.where(is_lat, x, xc_ref[rows, :])
        xn = x + g1_ref[0] * y
        xo_ref[rows, :] = xn
        h = _rms_scale(xn) * n2_ref[...]
        h = h * (1.0 + sc_ref[0]) + sh_ref[0]
        hb = h.astype(BF16)
        lo = lax.bitcast_convert_type(hb[:, :half].astype(F32), U32) >> 16
        hi = lax.bitcast_convert_type(hb[:, half:].astype(F32), U32) & jnp.uint32(0xFFFF0000)
        h_ref[rows, :] = lo | hi
        h_lo = (h - hb.astype(F32)).astype(BF16)
        both = jnp.dot(hb, wr_ref[...], preferred_element_type=F32)
        logits = (both[:, :n_exp] + both[:, n_exp:]
                  + jnp.dot(h_lo, wr_ref[:, :n_exp], preferred_element_type=F32))
        e = jnp.exp(logits - logits.max(axis=-1, keepdims=True))
        aff_ref[rows, :] = e / jnp.sum(e, axis=-1, keepdims=True)


def _merge(o_a, o_c, o_ax, o_cx, o_b, out_g, w_out_bf, xs, modt, norm2_g, w_router, rows_out, dims):
    n_batch, seq, ctx_len, d = dims
    tm = _pick_tile(512, seq, n_batch * ctx_len)
    n_lat_tiles = n_batch * seq // tm
    tps = seq // tm
    n_exp = w_router.shape[1]
    mix = A_WIDTH + B_WIDTH + C_WIDTH
    has_ctx = o_ax is not None
    assert has_ctx == (rows_out > n_batch * seq)
    split_x = isinstance(xs, tuple)
    assert has_ctx or not split_x
    w_hi = w_router.astype(BF16)
    w_lo = (w_router - w_hi.astype(F32)).astype(BF16)

    def mod_spec(k):
        return pl.BlockSpec((1, 1, d), lambda i: (_row_seg(i, n_lat_tiles, tps, n_batch) * N_MOD + k, 0, 0))

    def rows(width):
        return pl.BlockSpec((tm, width), lambda i: (i, 0))

    def lat_rows(width):
        return pl.BlockSpec((tm, width), lambda i: (jnp.minimum(i, n_lat_tiles - 1), 0))

    def ctx_rows(width):
        return pl.BlockSpec((tm, width), lambda i: (jnp.maximum(i - n_lat_tiles, 0), 0))

    def full(shape):
        return pl.BlockSpec(shape, lambda i: (0,) * len(shape))

    attn_specs = [lat_rows(A_WIDTH), lat_rows(C_WIDTH)]
    attn_args = [o_a, o_c]
    if has_ctx:
        attn_specs += [ctx_rows(A_WIDTH), ctx_rows(C_WIDTH)]
        attn_args += [o_ax, o_cx]
    if split_x:
        attn_specs += [lat_rows(d), ctx_rows(d)]
        attn_args += list(xs)
    else:
        attn_specs += [rows(d)]
        attn_args += [xs]
    return pl.pallas_call(
        functools.partial(_merge_kernel, n_lat_tiles=n_lat_tiles, has_ctx=has_ctx, split_x=split_x,
                          sub=min(256, tm)),
        grid=(rows_out // tm,),
        in_specs=attn_specs + [rows(B_WIDTH), full((1, mix)), full((mix, d)),
                               mod_spec(2), mod_spec(3), mod_spec(4), full((1, d)), full((d, 2 * n_exp))],
        out_specs=[rows(d), rows(d // 2), rows(n_exp)],
        out_shape=[jax.ShapeDtypeStruct((rows_out, d), F32), jax.ShapeDtypeStruct((rows_out, d // 2), U32),
                   jax.ShapeDtypeStruct((rows_out, n_exp), F32)],
        compiler_params=_params(("arbitrary",)),
        name="merge_router",
    )(*attn_args, o_b, out_g, w_out_bf, modt, modt, modt, norm2_g, jnp.concatenate([w_hi, w_lo], axis=1))


def _prefix_excl(mask_f32, tri):
    rows, n = mask_f32.shape
    cw = tri.shape[0]
    carry = jnp.zeros((rows, 1), F32)
    parts = []
    for c in range(n // cw):
        m = mask_f32[:, c * cw:(c + 1) * cw]
        parts.append(jnp.dot(m.astype(BF16), tri, preferred_element_type=F32) + carry)
        carry = carry + jnp.sum(m, axis=-1, keepdims=True)
    return parts[0] if len(parts) == 1 else jnp.concatenate(parts, axis=-1)


def _route_kernel(aff_ref, idx_ref, gate_ref, *, n, cap, n_exp, cb):
    a = aff_ref[0]
    bits = pltpu.bitcast(a, I32)

    def bit_step(b, thr):
        cand = thr | jnp.left_shift(jnp.int32(1), 30 - b)
        cnt = jnp.sum(jnp.where(bits >= cand, 1.0, 0.0), axis=-1, keepdims=True)
        return jnp.where(cnt >= cap, cand, thr)

    thr = lax.fori_loop(0, 31, bit_step, jnp.zeros((n_exp, 1), I32))
    gt = bits > thr
    eq = bits == thr
    cw = min(256, n)
    tri = jnp.where(lax.broadcasted_iota(I32, (cw, cw), 0) < lax.broadcasted_iota(I32, (cw, cw), 1),
                    1.0, 0.0).astype(BF16)
    need = cap - jnp.sum(jnp.where(gt, 1.0, 0.0), axis=-1, keepdims=True)
    sel = gt | (eq & (_prefix_excl(jnp.where(eq, 1.0, 0.0), tri) < need))
    pos = jnp.where(sel, _prefix_excl(jnp.where(sel, 1.0, 0.0), tri), -1.0)

    tok = lax.broadcasted_iota(I32, (1, n), 1)
    t_hi = (tok // 64).astype(F32)
    t_lo = (tok % 64).astype(F32)
    g_hi = a.astype(BF16).astype(F32)
    r1 = a - g_hi
    g_mid = r1.astype(BF16).astype(F32)
    g_lo = r1 - g_mid
    ca = cap // cb
    pa = jnp.floor(pos * (1.0 / cb))
    pb = pos - pa * cb
    n_val = 5
    rows_l = -(-n_val * ca // 16) * 16
    rowi = lax.broadcasted_iota(I32, (rows_l, n), 0)
    which = rowi // ca
    pa_row = (rowi % ca).astype(F32)
    pb_row = lax.broadcasted_iota(I32, (cb, n), 0).astype(F32)
    for e in range(n_exp):
        one = slice(e, e + 1)
        val = jnp.where(which == 0, t_hi,
              jnp.where(which == 1, t_lo,
              jnp.where(which == 2, g_hi[one, :],
              jnp.where(which == 3, g_mid[one, :],
              jnp.where(which == 4, g_lo[one, :], 0.0)))))
        lhs = jnp.where(pa[one, :] == pa_row, val, 0.0).astype(BF16)
        rhs = jnp.where(pb[one, :] == pb_row, 1.0, 0.0).astype(BF16)
        res = _dot_nt(lhs, rhs)
        part = lambda k: res[k * ca:(k + 1) * ca, :]
        idx_ref[0, e * ca:(e + 1) * ca, :] = (part(0) * 64.0 + part(1)).astype(I32)
        gate_ref[0, e * ca:(e + 1) * ca, :] = part(2) + part(3) + part(4)


def _route(aff_sets):
    n_sets, n_exp, n = aff_sets.shape
    cap = EC_CAPACITY * n // n_exp
    cb = min(32, cap)
    assert cap % cb == 0 and cb & (cb - 1) == 0
    spec = pl.BlockSpec((1, n_exp * cap // cb, cb), lambda s: (s, 0, 0))
    idx, gate = pl.pallas_call(
        functools.partial(_route_kernel, n=n, cap=cap, n_exp=n_exp, cb=cb),
        grid=(n_sets,),
        in_specs=[pl.BlockSpec((1, n_exp, n), lambda s: (s, 0, 0))],
        out_specs=[spec, spec],
        out_shape=[jax.ShapeDtypeStruct((n_sets, n_exp * cap // cb, cb), I32),
                   jax.ShapeDtypeStruct((n_sets, n_exp * cap // cb, cb), F32)],
        compiler_params=_params(("arbitrary",)),
        name="route",
    )(aff_sets)
    return idx.reshape(n_sets, n_exp, cap), gate.reshape(n_sets, n_exp, cap)


def _ffn_kernel(idx_ref, hpk_hbm, gate_ref, g2_ref, wg_ref, wu_ref, wd_ref, x_in_hbm, x_hbm,
                hpk, hb, yacc, xbuf, sem_h, sem_x, sem_s, *, m, n_f, n_groups, groups, n_split, per, ctx_seg):
    del x_in_hbm
    e = pl.program_id(0)
    part = pl.program_id(1)
    f = pl.program_id(2)
    g = e * n_split + part
    slot = g % 2
    g_prev = jnp.maximum(g - 1, 0)
    g_next = (g + 1) % n_groups
    half = n_f // 2
    rows_rw = m // half
    h_steps = max(n_f - 1, 1)
    h_base = -(-m // (h_steps * SUBLANES)) * SUBLANES
    h_parts = [(min(k * h_base, m), min((k + 1) * h_base, m)) for k in range(h_steps)]
    x_steps = list(range(half, n_f - 1)) or [n_f - 1]
    fused_tail = n_f - 1 not in x_steps
    x_base = -(-m // (len(x_steps) * SUBLANES)) * SUBLANES
    x_parts = {st: (min(k * x_base, m), min((k + 1) * x_base, m)) for k, st in enumerate(x_steps)}

    def h_copy(r, j, dst_slot):
        return pltpu.make_async_copy(hpk_hbm.at[pl.ds(r, 1), :], hpk.at[dst_slot, pl.ds(j, 1), :],
                                     sem_h.at[dst_slot])

    def x_copy(r, j):
        return pltpu.make_async_copy(x_hbm.at[pl.ds(r, 1), :], xbuf.at[pl.ds(j, 1), :], sem_x)

    def s_copy(r, j):
        return pltpu.make_async_copy(xbuf.at[pl.ds(j, 1), :], x_hbm.at[pl.ds(r, 1), :], sem_s)

    def start_rows(copy, grp, first, n, both_queues=False):
        for j in range(n):
            copy(idx_ref[grp * m + first + j], first + j).start(priority=j % 2 if both_queues else 0)

    def wait_h(s):
        pltpu.make_async_copy(hpk.at[1 - s], hpk.at[s], sem_h.at[s]).wait()

    def wait_rows(sem):
        pltpu.make_async_copy(yacc, xbuf, sem).wait()

    def for_rows(fn):
        def body(j, carry):
            fn(j)
            return carry
        lax.fori_loop(0, m, body, 0)

    @pl.when((g == 0) & (f == 0))
    def _():
        for_rows(lambda j: h_copy(idx_ref[j], j, 0).start())
        for_rows(lambda j: x_copy(idx_ref[j], j).start())
        wait_rows(sem_x)

    @pl.when(f == 0)
    def _():
        wait_h(slot)
        p = hpk[slot]
        d_half = hpk.shape[2]
        hb[:, :d_half] = lax.bitcast_convert_type(p << 16, F32).astype(BF16)
        hb[:, d_half:] = lax.bitcast_convert_type(p & jnp.uint32(0xFFFF0000), F32).astype(BF16)

    @pl.when(f == half)
    def _():
        wait_rows(sem_s)

    def accumulate_into_rows(total):
        for off, size, sample in groups:
            seg = ctx_seg if sample is None else part * per + sample
            rows = pl.ds(off, size)
            xbuf[rows, :] = xbuf[rows, :] + g2_ref[pl.ds(seg, 1), :] * (gate_ref[0, 0, rows, :]
                                                                      * total[off:off + size, :])

    def matmuls(first, last):
        h = hb[...]
        a = jnp.dot(h, wg_ref[...].astype(BF16), preferred_element_type=F32)
        u = jnp.dot(h, wu_ref[...].astype(BF16), preferred_element_type=F32)
        z = (a / (1.0 + jnp.exp(-a)) * u).astype(BF16)
        y = jnp.dot(z, wd_ref[...].astype(BF16), preferred_element_type=F32)
        if last:
            accumulate_into_rows(yacc[...] + y)
        elif first:
            yacc[...] = y
        else:
            yacc[...] += y

    for step in range(n_f):
        @pl.when(f == step)
        def _(step=step):
            if step < half:
                start_rows(s_copy, g_prev, step * rows_rw, rows_rw, both_queues=True)
            if step in x_parts:
                start_rows(x_copy, g, x_parts[step][0], x_parts[step][1] - x_parts[step][0])
            if fused_tail and step == n_f - 1:
                wait_rows(sem_x)
            if step < h_steps:
                start_rows(lambda r, j: h_copy(r, j, 1 - slot), g_next, h_parts[step][0],
                           h_parts[step][1] - h_parts[step][0])
            matmuls(first=step == 0, last=fused_tail and step == n_f - 1)

    if not fused_tail:
        @pl.when(f == n_f - 1)
        def _():
            wait_rows(sem_x)
            accumulate_into_rows(yacc[...])

    @pl.when((g == n_groups - 1) & (f == n_f - 1))
    def _():
        for_rows(lambda j: s_copy(idx_ref[g * m + j], j).start())
        wait_rows(sem_s)
        wait_h(1 - slot)


def _experts(idx_flat, gate, hpk, g2tab, w_gate, w_up, w_down, xs, layer, m, groups, n_split, per, ctx_seg):
    d = xs.shape[1]
    n_exp, d_exp = w_gate.shape[1], w_gate.shape[3]
    tf = _pick_tile(256, d_exp // 2)
    n_f = d_exp // tf
    n_groups = n_exp * n_split
    assert n_f % 2 == 0 and m % (n_f // 2) == 0 and n_groups % 2 == 0
    grid_spec = pltpu.PrefetchScalarGridSpec(
        num_scalar_prefetch=1,
        grid=(n_exp, n_split, n_f),
        in_specs=[
            pl.BlockSpec(memory_space=pl.ANY),
            pl.BlockSpec((1, 1, m, 1), lambda e, p, f, idx: (e, p, 0, 0)),
            pl.BlockSpec(g2tab.shape, lambda e, p, f, idx: (0, 0)),
            pl.BlockSpec((None, None, d, tf), lambda e, p, f, idx: (layer, e, 0, f)),
            pl.BlockSpec((None, None, d, tf), lambda e, p, f, idx: (layer, e, 0, f)),
            pl.BlockSpec((None, None, tf, d), lambda e, p, f, idx: (layer, e, f, 0)),
            pl.BlockSpec(memory_space=pl.ANY),
        ],
        out_specs=pl.BlockSpec(memory_space=pl.ANY),
        scratch_shapes=[pltpu.VMEM((2, m, d // 2), U32), pltpu.VMEM((m, d), BF16), pltpu.VMEM((m, d), F32),
                        pltpu.VMEM((m, d), F32), pltpu.SemaphoreType.DMA((2,)), pltpu.SemaphoreType.DMA,
                        pltpu.SemaphoreType.DMA],
    )
    return pl.pallas_call(
        functools.partial(_ffn_kernel, m=m, n_f=n_f, n_groups=n_groups, groups=tuple(groups), n_split=n_split,
                          per=per, ctx_seg=ctx_seg),
        grid_spec=grid_spec,
        out_shape=jax.ShapeDtypeStruct(xs.shape, xs.dtype),
        input_output_aliases={7: 0},
        compiler_params=_params(("arbitrary",) * 3),
        name="experts",
    )(idx_flat, hpk, gate, g2tab, w_gate, w_up, w_down, xs)


def _rope_tables(seq):
    pos = np.arange(seq)
    n_freq = HEAD_DIM // 4
    inv = jnp.asarray(ROPE_THETA, F32) ** (-jnp.arange(n_freq, dtype=F32) / n_freq)
    row = jnp.asarray(pos // GRID_W, F32)[:, None] * inv
    col = jnp.asarray(pos % GRID_W, F32)[:, None] * inv
    zero = jnp.zeros_like(row)
    cos = jnp.concatenate([jnp.cos(row), jnp.cos(row), jnp.cos(col), jnp.cos(col)], axis=-1)
    s1 = jnp.concatenate([zero, jnp.sin(row), zero, jnp.sin(col)], axis=-1)
    s2 = jnp.concatenate([-jnp.sin(row), zero, -jnp.sin(col), zero], axis=-1)

    def padded(tm):
        ident = jnp.ones((tm, HEAD_DIM), F32)
        none = jnp.zeros((tm, HEAD_DIM), F32)
        return (jnp.concatenate([cos, ident], axis=0), jnp.concatenate([s1, none], axis=0),
                jnp.concatenate([s2, none], axis=0))

    return padded


def _expert_rows(idx_l, gate_l, idx_c, gate_c, dims, n_split):
    n_batch, seq, ctx_len, _ = dims
    per = n_batch // n_split
    n_exp = idx_l.shape[1]

    def arrange(v):
        s = v.shape[-1]
        return v.reshape(n_split, per, n_exp, s).transpose(2, 0, 1, 3).reshape(n_exp, n_split, per * s)

    rows_l = idx_l + (jnp.arange(n_batch, dtype=I32) * seq)[:, None, None]
    parts_i, parts_g = [arrange(rows_l)], [arrange(gate_l)]
    groups = [(b * idx_l.shape[-1], idx_l.shape[-1], b) for b in range(per)]
    if idx_c is not None:
        rows_c = idx_c + (n_batch * seq + jnp.arange(n_batch, dtype=I32) * ctx_len)[:, None, None]
        parts_i.append(arrange(rows_c))
        parts_g.append(arrange(gate_c))
        groups += [(per * idx_l.shape[-1], per * idx_c.shape[-1], None)]
    idx = jnp.concatenate(parts_i, axis=-1)
    gate = jnp.concatenate(parts_g, axis=-1)
    return idx, gate, groups


def kernel(x, c, ctx, c_ctx, w_mod, b_mod, norm1_g, norm2_g, w_in, qn_a, kn_a, sink_a, vn_b, w_s, b_s,
           qn_c, kn_c, out_g, w_out, w_router, w_gate, w_up, w_down):
    n_batch, seq, d = x.shape
    ctx_len = ctx.shape[1]
    depth = w_mod.shape[0]
    n_exp = w_router.shape[2]
    dims = (n_batch, seq, ctx_len, d)
    t_lat = n_batch * seq
    assert n_batch < MOD_ROWS and seq % WINDOW == 0 and ctx_len % B_CHUNK == 0 and t_lat % ctx_len == 0
    n_split = 2 if n_batch % 2 == 0 else 1
    per = n_batch // n_split

    xs = (x.reshape(t_lat, d), ctx.reshape(n_batch * ctx_len, d))
    cvec = jnp.zeros((MOD_ROWS, d), F32).at[:n_batch].set(c).at[n_batch].set(c_ctx)
    rope = _rope_tables(seq)

    for layer in range(depth):
        last = layer == depth - 1
        mod = _modulation(cvec, w_mod, b_mod, layer)
        modt = mod.reshape(MOD_ROWS * N_MOD, 1, d)
        g2tab = mod.reshape(MOD_ROWS, N_MOD, d)[:, N_MOD - 1, :]
        row = lambda v: v[layer].reshape(1, -1)
        bsb = jnp.broadcast_to(b_s[layer][:, :, None], (B_GROUPS, B_CHUNK, HEAD_DIM))
        qa, ka, va, qc, kc, vc, o_b = _inproj(
            xs, modt, row(norm1_g), w_in[layer].astype(BF16), rope, row(qn_a), row(kn_a), row(qn_c),
            row(kn_c), row(vn_b), w_s[layer].astype(BF16), bsb, dims)
        o_a = _attn_win(sink_a[layer], qa, ka, va, dims)
        o_c = _attn_glob(qc, kc, vc, dims)
        o_ax = o_cx = None
        if not last:
            o_ax = _attn_ctx(sink_a[layer], qa, ka, va, A_HEADS, A_KV, True, dims)
            o_cx = _attn_ctx(sink_a[layer], qc, kc, vc, C_HEADS, C_KV, False, dims)
        rows_out = t_lat if last else t_lat + n_batch * ctx_len
        xs, h2, aff = _merge(o_a, o_c, o_ax, o_cx, o_b, row(out_g), w_out[layer].astype(BF16), xs, modt,
                             row(norm2_g), w_router[layer], rows_out, dims)

        idx_l, gate_l = _route(aff[:t_lat].reshape(n_batch, seq, n_exp).transpose(0, 2, 1))
        idx_c = gate_c = None
        if not last:
            idx_c, gate_c = _route(aff[t_lat:].reshape(n_batch, ctx_len, n_exp).transpose(0, 2, 1))
        idx, gate, groups = _expert_rows(idx_l, gate_l, idx_c, gate_c, dims, n_split)
        m = idx.shape[-1]
        xs = _experts(idx.reshape(-1), gate.reshape(n_exp, n_split, m, 1), h2, g2tab, w_gate, w_up, w_down,
                      xs, layer, m, groups, n_split, per, n_batch)
    return xs.reshape(n_batch, seq, d)
```
